```python
import jax, jax.numpy as jnp
from jax import lax
import numpy as np

D_MODEL = 2048
BATCH = 32
SEQ = 256
DEPTH = 4
DEC_BATCH = 4
DEC_SEQ = 2048
PAST_LEN = 256

GRID_W = 64
N_EVEN = (DEPTH + 1) // 2
N_ODD = DEPTH // 2
H_A = 8
DH_A = 128
NA = H_A * DH_A
WIN_R = 8
WIN_C = 16
NB = 1024
CONV_W = 3
IN_AB = 3 * NA + 3 * NB
H_C = 16
Q_LORA = 512
KV_LORA = 256
NOPE = 128
ROPE = 64
V_DIM = 128
QK_DIM = NOPE + ROPE
ROPE_BASE = 10000.0
D_FF = -(-8 * D_MODEL // (3 * 256)) * 256
Q_BLOCK = 128
EPS = 1e-6

kernel_name = "hybrid_nat_conv_mla_diffusion_step"


def rms_norm(x, g):
    xf = x.astype(jnp.float32)
    y = xf * lax.rsqrt(jnp.mean(xf * xf, axis=-1, keepdims=True) + EPS)
    return (y * g.astype(jnp.float32)).astype(x.dtype)


def adaln(cond, w, b):
    m = jax.nn.silu(cond) @ w + b
    return jnp.split(m[..., None, :], 6, axis=-1)


def modulate(x, g, shift, scale):
    return rms_norm(x, g) * (1.0 + scale) + shift


def _heads(t, n_heads):
    b, l, _ = t.shape
    return t.reshape(b, l, n_heads, -1).transpose(0, 2, 1, 3)


def _merge(t):
    b, n, l, d = t.shape
    return t.transpose(0, 2, 1, 3).reshape(b, l, n * d)


def attend(q, k, v, scale):
    b, h, lq, dq = q.shape
    nb = lq // Q_BLOCK
    qb = q.reshape(b, h, nb, Q_BLOCK, dq).transpose(2, 0, 1, 3, 4)

    def one(qi):
        s = jnp.einsum('bhqd,bhkd->bhqk', qi, k, preferred_element_type=jnp.float32) * scale
        p = jax.nn.softmax(s, axis=-1).astype(v.dtype)
        return jnp.einsum('bhqk,bhkd->bhqd', p, v)

    o = lax.map(one, qb)
    return o.transpose(1, 2, 0, 3, 4).reshape(b, h, lq, -1)


def axial_rope(x, n_tok):
    t = jnp.arange(n_tok)
    half = x.shape[-1] // 2
    freqs = ROPE_BASE ** (-jnp.arange(half // 2, dtype=jnp.float32) / (half // 2))

    def rot(xa, pos):
        ang = pos.astype(jnp.float32)[:, None] * freqs
        cos, sin = jnp.cos(ang), jnp.sin(ang)
        x1, x2 = jnp.split(xa.astype(jnp.float32), 2, axis=-1)
        return jnp.concatenate([x1 * cos - x2 * sin, x1 * sin + x2 * cos], axis=-1)

    xr, xc = jnp.split(x, 2, axis=-1)
    return jnp.concatenate([rot(xr, t // GRID_W), rot(xc, t % GRID_W)], axis=-1).astype(x.dtype)


def short_conv(u, w):
    return lax.conv_general_dilated(u, w[:, None, :], window_strides=(1,), padding=((1, 1),),
                                    dimension_numbers=('NWC', 'WIO', 'NWC'),
                                    feature_group_count=u.shape[-1])


def ab_split(h, w_in, g_qn, g_kn, conv_w):
    proj = h @ w_in
    q, k, v, gb, gc, u = jnp.split(proj, [NA, 2 * NA, 3 * NA, 3 * NA + NB, 3 * NA + 2 * NB], axis=-1)
    q = rms_norm(_heads(q, H_A), g_qn)
    k = rms_norm(_heads(k, H_A), g_kn)
    v = _heads(v, H_A)
    y_b = gb * short_conv(gc * u, conv_w)
    return q, k, v, y_b


def neighbourhood_attention(q, k, v, ctx_k, ctx_v, rpb):
    b, h, n, dh = q.shape
    rows = n // GRID_W
    kr = min(WIN_R, rows)
    r = jnp.arange(rows)
    rs = jnp.clip(r - kr // 2, 0, rows - kr)
    key_rows = rs[:, None] + jnp.arange(kr)
    j = jnp.arange(GRID_W)
    cs = jnp.clip(j - WIN_C // 2, 0, GRID_W - WIN_C)
    col_in = (j[None, :] >= cs[:, None]) & (j[None, :] < cs[:, None] + WIN_C)
    qg = q.reshape(b, h, rows, GRID_W, dh)
    kg = k.reshape(b, h, rows, GRID_W, dh)[:, :, key_rows]
    vg = v.reshape(b, h, rows, GRID_W, dh)[:, :, key_rows]
    scale = dh ** -0.5
    s_loc = jnp.einsum('bhrqd,bhrikd->bhrqik', qg, kg, preferred_element_type=jnp.float32) * scale
    ro = key_rows - r[:, None] + (WIN_R - 1)
    co = jnp.clip(j[None, :] - j[:, None] + (WIN_C - 1), 0, 2 * WIN_C - 2)
    bias = rpb[:, ro[:, None, :, None], co[None, :, None, :]].astype(jnp.float32)
    s_loc = jnp.where(col_in[None, None, None, :, None, :], s_loc + bias[None], -jnp.inf)
    s_ctx = jnp.einsum('bhrqd,bhcd->bhrqc', qg, ctx_k, preferred_element_type=jnp.float32) * scale
    lw = kr * GRID_W
    s = jnp.concatenate([s_loc.reshape(b, h, rows, GRID_W, lw), s_ctx], axis=-1)
    p = jax.nn.softmax(s, axis=-1).astype(v.dtype)
    o = (jnp.einsum('bhrqik,bhrikd->bhrqd', p[..., :lw].reshape(b, h, rows, GRID_W, kr, GRID_W), vg)
         + jnp.einsum('bhrqc,bhcd->bhrqd', p[..., lw:], ctx_v))
    return o.reshape(b, h, n, dh)


def mla_latents(h, w_down, g_cq, g_ckv):
    cq, ckv, kr = jnp.split(h @ w_down, [Q_LORA, Q_LORA + KV_LORA], axis=-1)
    return rms_norm(cq, g_cq), rms_norm(ckv, g_ckv), kr


def mla_q(cq, w_uq, g_qn):
    return rms_norm(_heads(cq @ w_uq, H_C), g_qn)


def mla_kv(ckv, kr, w_ukv, g_kn):
    kv = _heads(ckv @ w_ukv, H_C)
    k_nope, v = jnp.split(kv, [NOPE], axis=-1)
    k_r = jnp.broadcast_to(kr[:, None], k_nope.shape[:-1] + (ROPE,))
    k = rms_norm(jnp.concatenate([k_nope, k_r], axis=-1), g_kn)
    return k, v


def rope_part(t):
    return jnp.concatenate([t[..., :NOPE], axial_rope(t[..., NOPE:], t.shape[-2])], axis=-1)


def swiglu(h, w_in, w_out):
    gate, up = jnp.split(h @ w_in, 2, axis=-1)
    return (jax.nn.silu(gate) * up) @ w_out


def setup_inputs(seed: int = 0) -> dict:
    key = jax.random.key(seed)
    ks = iter(jax.random.split(key, 32))

    def nrm(shape, scale):
        return jax.random.normal(next(ks), shape, jnp.float32) * scale

    def gain(shape):
        return 1.0 + nrm(shape, 0.02)

    D = D_MODEL
    return {
        'x_prompt': nrm((BATCH, SEQ, D), 1.0),
        'x_sample': nrm((DEC_BATCH, DEC_SEQ, D), 1.0),
        'cache_nat_k': nrm((DEC_BATCH, N_EVEN, H_A, PAST_LEN, DH_A), 1.0),
        'cache_nat_v': nrm((DEC_BATCH, N_EVEN, H_A, PAST_LEN, DH_A), 1.0),
        'cache_mla_ckv': nrm((DEC_BATCH, N_ODD, PAST_LEN, KV_LORA), 1.0),
        'cache_mla_krope': nrm((DEC_BATCH, N_ODD, PAST_LEN, ROPE), 1.0),
        'c': nrm((DEC_BATCH, D), 1.0),
        'c_ctx': nrm((D,), 1.0),
        'norm1_g': gain((DEPTH, D)),
        'norm2_g': gain((DEPTH, D)),
        'w_ada': nrm((DEPTH, D, 6 * D), 0.5 * D ** -0.5),
        'b_ada': nrm((DEPTH, 6 * D), 0.01),
        'w_in_ab': nrm((N_EVEN, D, IN_AB), D ** -0.5),
        'g_qn_a': gain((N_EVEN, DH_A)),
        'g_kn_a': gain((N_EVEN, DH_A)),
        'rpb_a': nrm((N_EVEN, H_A, 2 * WIN_R - 1, 2 * WIN_C - 1), 0.1),
        'conv_b_w': nrm((N_EVEN, CONV_W, NB), CONV_W ** -0.5),
        'w_out_ab': nrm((N_EVEN, NA + NB, D), (NA + NB) ** -0.5),
        'w_down_c': nrm((N_ODD, D, Q_LORA + KV_LORA + ROPE), D ** -0.5),
        'g_cq': gain((N_ODD, Q_LORA)),
        'g_ckv': gain((N_ODD, KV_LORA)),
        'w_uq_c': nrm((N_ODD, Q_LORA, H_C * QK_DIM), Q_LORA ** -0.5),
        'w_ukv_c': nrm((N_ODD, KV_LORA, H_C * (NOPE + V_DIM)), KV_LORA ** -0.5),
        'g_qn_c': gain((N_ODD, QK_DIM)),
        'g_kn_c': gain((N_ODD, QK_DIM)),
        'w_o_c': nrm((N_ODD, H_C * V_DIM, D), (H_C * V_DIM) ** -0.5),
        'w_ffn_in': nrm((DEPTH, D, 2 * D_FF), D ** -0.5),
        'w_ffn_out': nrm((DEPTH, D_FF, D), D_FF ** -0.5),
    }


def reference(x_prompt, x_sample, cache_nat_k, cache_nat_v, cache_mla_ckv, cache_mla_krope, c, c_ctx,
              norm1_g, norm2_g, w_ada, b_ada, w_in_ab, g_qn_a, g_kn_a, rpb_a, conv_b_w, w_out_ab,
              w_down_c, g_cq, g_ckv, w_uq_c, w_ukv_c, g_qn_c, g_kn_c, w_o_c, w_ffn_in, w_ffn_out):
    xc = x_prompt
    xs = x_sample
    nat_k, nat_v, mla_ckv, mla_kr = [], [], [], []
    for l in range(DEPTH):
        i = l // 2
        mc = adaln(c_ctx, w_ada[l], b_ada[l])
        ms = adaln(c, w_ada[l], b_ada[l])
        hc = modulate(xc, norm1_g[l], mc[0], mc[1])
        hs = modulate(xs, norm1_g[l], ms[0], ms[1])
        if l % 2 == 0:
            q, k, v, yb = ab_split(hc, w_in_ab[i], g_qn_a[i], g_kn_a[i], conv_b_w[i])
            ya = attend(q, k, v, DH_A ** -0.5)
            yc = jnp.concatenate([_merge(ya), yb], axis=-1) @ w_out_ab[i]
            nat_k.append(k)
            nat_v.append(v)
            q, k, v, yb = ab_split(hs, w_in_ab[i], g_qn_a[i], g_kn_a[i], conv_b_w[i])
            ya = neighbourhood_attention(q, k, v, cache_nat_k[:, i], cache_nat_v[:, i], rpb_a[i])
            ys = jnp.concatenate([_merge(ya), yb], axis=-1) @ w_out_ab[i]
        else:
            scale = QK_DIM ** -0.5
            cq, ckv, kr = mla_latents(hc, w_down_c[i], g_cq[i], g_ckv[i])
            q = mla_q(cq, w_uq_c[i], g_qn_c[i])
            k, v = mla_kv(ckv, kr, w_ukv_c[i], g_kn_c[i])
            yc = _merge(attend(q, k, v, scale)) @ w_o_c[i]
            mla_ckv.append(ckv)
            mla_kr.append(kr)
            cq, ckv, kr = mla_latents(hs, w_down_c[i], g_cq[i], g_ckv[i])
            q = rope_part(mla_q(cq, w_uq_c[i], g_qn_c[i]))
            k, v = mla_kv(ckv, kr, w_ukv_c[i], g_kn_c[i])
            k = rope_part(k)
            kx, vx = mla_kv(cache_mla_ckv[:, i], cache_mla_krope[:, i], w_ukv_c[i], g_kn_c[i])
            o = attend(q, jnp.concatenate([kx, k], axis=2), jnp.concatenate([vx, v], axis=2), scale)
            ys = _merge(o) @ w_o_c[i]
        xc = xc + mc[2] * yc
        xs = xs + ms[2] * ys
        xc = xc + mc[5] * swiglu(modulate(xc, norm2_g[l], mc[3], mc[4]), w_ffn_in[l], w_ffn_out[l])
        xs = xs + ms[5] * swiglu(modulate(xs, norm2_g[l], ms[3], ms[4]), w_ffn_in[l], w_ffn_out[l])
    return (xc, xs, jnp.stack(nat_k, axis=1), jnp.stack(nat_v, axis=1),
            jnp.stack(mla_ckv, axis=1), jnp.stack(mla_kr, axis=1))
```

```python
import functools

import numpy as np
import jax
import jax.numpy as jnp
from jax import lax
from jax.experimental import pallas as pl
from jax.experimental.pallas import tpu as pltpu

F32 = jnp.float32
BF16 = jnp.bfloat16

EPS = 1e-6
GRID_W = 64
WIN_R = 8
WIN_C = 16
ROPE_BASE = 10000.0
H_A = 8
DH_A = 128
H_C = 16
Q_LORA = 512
KV_LORA = 256
NOPE = 128
ROPE = 64
V_DIM = 128
QK_DIM = NOPE + ROPE
QK_PAD = 256
LANES = 128
SUBLANES = 8
NEG = -1e30

VMEM_LIMIT = 56 * 1024 * 1024


def _params(*sem):
    return pltpu.CompilerParams(dimension_semantics=sem, vmem_limit_bytes=VMEM_LIMIT)


def _group(i, nct, per):
    return jnp.where(i < nct, 0, 1 + (i - nct) // per)


def _modulated(x, g, shift, scale):
    ms = jnp.mean(x * x, axis=-1, keepdims=True)
    return (x * lax.rsqrt(ms + EPS) * g) * (1.0 + scale) + shift


def _dot(a, b):
    return jnp.dot(a, b, preferred_element_type=F32)


def _dot_nt(a, b):
    return lax.dot_general(a, b, (((1,), (1,)), ((), ())), preferred_element_type=F32)


def _ada_kernel(c_ref, w_ref, b_ref, o_ref):
    c = c_ref[...]
    s = (c / (1.0 + jnp.exp(-c))).astype(BF16)
    o_ref[...] = _dot(s, w_ref[...].astype(BF16)) + b_ref[...]


def _adaln(cond, w_ada, b_ada):
    depth, d, n6 = w_ada.shape
    gp = cond.shape[0]
    tn = 1024 if n6 % 1024 == 0 else n6
    return pl.pallas_call(
        _ada_kernel,
        out_shape=jax.ShapeDtypeStruct((depth, gp, n6), F32),
        grid=(depth, n6 // tn),
        in_specs=[
            pl.BlockSpec((gp, d), lambda l, n: (0, 0)),
            pl.BlockSpec((None, d, tn), lambda l, n: (l, 0, n)),
            pl.BlockSpec((None, 1, tn), lambda l, n: (l, 0, n)),
        ],
        out_specs=pl.BlockSpec((None, gp, tn), lambda l, n: (l, 0, n)),
        compiler_params=_params("arbitrary", "arbitrary"),
        name="adaln",
    )(cond, w_ada, b_ada.reshape(depth, 1, n6))


def _inproj_ab_kernel(x_ref, mod_ref, g1_ref, w_ref, gq_ref, gk_ref,
                      qkv_ref, gb_ref, z_ref, natk_ref, natv_ref,
                      h_ref, gc_ref, *, nct, seq):
    i = pl.program_id(0)
    n = pl.program_id(1)
    tm = x_ref.shape[0]

    @pl.when(n == 0)
    def _():
        h = _modulated(x_ref[...], g1_ref[...], mod_ref[0:1, :], mod_ref[1:2, :])
        h_ref[...] = h.astype(BF16)

    r = _dot(h_ref[...], w_ref[...])

    def head_norm(gain_ref, nat_ref):
        for hh in range(H_A):
            t = r[:, hh * DH_A:(hh + 1) * DH_A]
            ms = jnp.mean(t * t, axis=-1, keepdims=True)
            tn_ = t * lax.rsqrt(ms + EPS) * gain_ref[...]
            qkv_ref[:, hh * DH_A:(hh + 1) * DH_A] = tn_.astype(BF16)
            if nat_ref is not None:
                @pl.when(i < nct)
                def _():
                    for bb in range(tm // seq):
                        nat_ref[bb, hh] = tn_[bb * seq:(bb + 1) * seq, :]

    @pl.when(n == 0)
    def _():
        head_norm(gq_ref, None)

    @pl.when(n == 1)
    def _():
        head_norm(gk_ref, natk_ref)

    @pl.when(n == 2)
    def _():
        qkv_ref[...] = r.astype(BF16)

        @pl.when(i < nct)
        def _():
            for bb in range(tm // seq):
                for hh in range(H_A):
                    natv_ref[bb, hh] = r[bb * seq:(bb + 1) * seq, hh * DH_A:(hh + 1) * DH_A]

    @pl.when(n == 3)
    def _():
        gb_ref[...] = r

    @pl.when(n == 4)
    def _():
        gc_ref[...] = r

    @pl.when(n == 5)
    def _():
        z_ref[...] = gc_ref[...] * r


def _inproj_ab(x, mods, g1, w_in, gq, gk, *, nc, seq, dseq, batch, tm):
    n, d = x.shape
    na = H_A * DH_A
    nct = nc // tm
    per = dseq // tm
    kern = functools.partial(_inproj_ab_kernel, nct=nct, seq=seq)
    nat_shape = jax.ShapeDtypeStruct((batch, H_A, seq, DH_A), F32)
    nat_spec = pl.BlockSpec((tm // seq, H_A, seq, DH_A),
                            lambda i, j: (jnp.minimum(i, nct - 1), 0, 0, 0))
    return pl.pallas_call(
        kern,
        out_shape=(jax.ShapeDtypeStruct((n, 3 * na), BF16),
                   jax.ShapeDtypeStruct((n, na), F32),
                   jax.ShapeDtypeStruct((n, na), F32),
                   nat_shape, nat_shape),
        grid=(n // tm, 6),
        in_specs=[
            pl.BlockSpec((tm, d), lambda i, j: (i, 0)),
            pl.BlockSpec((None, 6, d), lambda i, j: (_group(i, nct, per), 0, 0)),
            pl.BlockSpec((1, d), lambda i, j: (0, 0)),
            pl.BlockSpec((d, na), lambda i, j: (0, j)),
            pl.BlockSpec((1, DH_A), lambda i, j: (0, 0)),
            pl.BlockSpec((1, DH_A), lambda i, j: (0, 0)),
        ],
        out_specs=(
            pl.BlockSpec((tm, na), lambda i, j: (i, jnp.minimum(j, 2))),
            pl.BlockSpec((tm, na), lambda i, j: (i, 0)),
            pl.BlockSpec((tm, na), lambda i, j: (i, 0)),
            nat_spec, nat_spec,
        ),
        scratch_shapes=[pltpu.VMEM((tm, d), BF16), pltpu.VMEM((tm, na), F32)],
        compiler_params=_params("arbitrary", "arbitrary"),
        name="inproj_ab",
    )(x, mods, g1, w_in, gq, gk)


def _ctx_attn_kernel(q_ref, k_ref, v_ref, o_ref, *, heads, dq, dv, scale):
    for hh in range(heads):
        q = q_ref[:, hh * dq:(hh + 1) * dq]
        k = k_ref[:, hh * dq:(hh + 1) * dq]
        v = v_ref[:, hh * dv:(hh + 1) * dv]
        s = _dot_nt(q, k) * scale
        m = jnp.max(s, axis=-1, keepdims=True)
        p = jnp.exp(s - m)
        l = jnp.sum(p, axis=-1, keepdims=True)
        o = _dot(p.astype(BF16), v) / l
        o_ref[:, hh * dv:(hh + 1) * dv] = o.astype(BF16)


def _ctx_attn(q_arr, k_arr, v_arr, cols, *, n, batch, seq, heads, dq, dv, scale):
    cq, ck, cv = cols
    kern = functools.partial(_ctx_attn_kernel, heads=heads, dq=dq, dv=dv, scale=scale)
    return pl.pallas_call(
        kern,
        out_shape=jax.ShapeDtypeStruct((n, heads * dv), BF16),
        grid=(batch,),
        in_specs=[
            pl.BlockSpec((seq, heads * dq), lambda b: (b, cq)),
            pl.BlockSpec((seq, heads * dq), lambda b: (b, ck)),
            pl.BlockSpec((seq, heads * dv), lambda b: (b, cv)),
        ],
        out_specs=pl.BlockSpec((seq, heads * dv), lambda b: (b, 0)),
        compiler_params=_params("arbitrary"),
        name="ctx_attn",
    )(q_arr, k_arr, v_arr)


NAT_QR = 8
NAT_KR = 16


def _nat_window(a, rows):
    return int(np.clip(a * NAT_QR - WIN_R // 2, 0, rows - NAT_KR))


def _nat_attn_kernel(rpb_ref, q_ref, k_ref, v_ref, ck_ref, cv_ref, ya_ref, o_ref,
                     t2_ref, bias_ref, *, rows, scale):
    del ya_ref
    hh = pl.program_id(0)
    b = pl.program_id(1)
    nblk = rows // NAT_QR
    n_dr = 2 * WIN_R - 1
    n_dc = 2 * WIN_C - 1

    @pl.when(b == 0)
    def _():
        j = lax.broadcasted_iota(jnp.int32, (GRID_W, LANES), 0)
        kc = lax.broadcasted_iota(jnp.int32, (GRID_W, LANES), 1) % GRID_W
        cs = jnp.clip(j - WIN_C // 2, 0, GRID_W - WIN_C)
        col_in = (kc >= cs) & (kc < cs + WIN_C)
        dc = kc - j + (WIN_C - 1)
        for d in range(n_dr):
            t = jnp.full((GRID_W, LANES), NEG, F32)
            for c in range(n_dc):
                t = jnp.where(col_in & (dc == c), rpb_ref[hh, d * n_dc + c], t)
            t2_ref[d] = t
        lane = lax.broadcasted_iota(jnp.int32, (GRID_W, LANES), 1)
        negt = jnp.full((GRID_W, LANES), NEG, F32)
        for a in range(nblk):
            ws = _nat_window(a, rows)
            for qr in range(NAT_QR):
                r = a * NAT_QR + qr
                rs = int(np.clip(r - WIN_R // 2, 0, rows - WIN_R))
                for pp in range(NAT_KR // 2):
                    halves = []
                    for kk in (2 * pp, 2 * pp + 1):
                        kr = ws + kk
                        halves.append(t2_ref[kr - r + WIN_R - 1] if rs <= kr < rs + WIN_R else None)
                    if halves[0] is None and halves[1] is None:
                        blk = negt
                    else:
                        left = negt if halves[0] is None else halves[0]
                        right = negt if halves[1] is None else halves[1]
                        blk = jnp.where(lane < GRID_W, left, right)
                    bias_ref[a, qr * GRID_W:(qr + 1) * GRID_W, pp * LANES:(pp + 1) * LANES] = blk

    ck = ck_ref[...].astype(BF16)
    cv = cv_ref[...].astype(BF16)
    tq = NAT_QR * GRID_W
    tk = NAT_KR * GRID_W
    for a in range(nblk):
        ws = _nat_window(a, rows)
        q = q_ref[a * tq:(a + 1) * tq, :]
        kw = k_ref[ws * GRID_W:ws * GRID_W + tk, :]
        vw = v_ref[ws * GRID_W:ws * GRID_W + tk, :]
        s_loc = _dot_nt(q, kw) * scale + bias_ref[a]
        s_ctx = _dot_nt(q, ck) * scale
        m = jnp.maximum(jnp.max(s_loc, axis=-1, keepdims=True),
                        jnp.max(s_ctx, axis=-1, keepdims=True))
        p_loc = jnp.exp(s_loc - m)
        p_ctx = jnp.exp(s_ctx - m)
        l = jnp.sum(p_loc, axis=-1, keepdims=True) + jnp.sum(p_ctx, axis=-1, keepdims=True)
        o = (_dot(p_loc.astype(BF16), vw) + _dot(p_ctx.astype(BF16), cv)) / l
        o_ref[a * tq:(a + 1) * tq, :] = o.astype(BF16)


def _nat_attn(rpb, qkv, cache_k, cache_v, ya, slot, *, nc, dseq, dbatch):
    rows = dseq // GRID_W
    assert rows % NAT_QR == 0 and rows >= NAT_KR
    nblk = rows // NAT_QR
    past = cache_k.shape[3]
    off = nc // dseq
    n_dr, n_dc = 2 * WIN_R - 1, 2 * WIN_C - 1
    kern = functools.partial(_nat_attn_kernel, rows=rows, scale=DH_A ** -0.5)
    cache_spec = pl.BlockSpec((None, None, None, past, DH_A), lambda h, b: (b, slot, h, 0, 0))
    return pl.pallas_call(
        kern,
        out_shape=jax.ShapeDtypeStruct(ya.shape, ya.dtype),
        grid=(H_A, dbatch),
        in_specs=[
            pl.BlockSpec(memory_space=pltpu.SMEM),
            pl.BlockSpec((dseq, DH_A), lambda h, b: (off + b, h)),
            pl.BlockSpec((dseq, DH_A), lambda h, b: (off + b, H_A + h)),
            pl.BlockSpec((dseq, DH_A), lambda h, b: (off + b, 2 * H_A + h)),
            cache_spec, cache_spec,
            pl.BlockSpec(memory_space=pl.ANY),
        ],
        out_specs=pl.BlockSpec((dseq, DH_A), lambda h, b: (off + b, h)),
        scratch_shapes=[pltpu.VMEM((n_dr, GRID_W, LANES), F32),
                        pltpu.VMEM((nblk, NAT_QR * GRID_W, NAT_KR * GRID_W), F32)],
        input_output_aliases={6: 0},
        compiler_params=_params("arbitrary", "arbitrary"),
        name="nat_attn",
    )(rpb.reshape(H_A, n_dr * n_dc), qkv, qkv, qkv, cache_k, cache_v, ya)


def _outproj_kernel(*refs, conv, nct, seq, dseq):
    if conv:
        (x_ref, mod_ref, ya_ref, gb_ref, z_ref, zp_ref, zn_ref, cw_ref, w_ref,
         o_ref, zs_ref) = refs
    else:
        x_ref, mod_ref, ya_ref, w_ref, o_ref = refs
    i = pl.program_id(0)
    tm = x_ref.shape[0]
    ka = ya_ref.shape[1]
    y = _dot(ya_ref[...], w_ref[0:ka, :])
    if conv:
        h8 = SUBLANES
        zs_ref[0:h8, :] = zp_ref[...]
        zs_ref[h8:h8 + tm, :] = z_ref[...]
        zs_ref[h8 + tm:2 * h8 + tm, :] = zn_ref[...]
        row = lax.broadcasted_iota(jnp.int32, (tm, 1), 0)
        pos = jnp.where(i < nct, row % seq, ((i - nct) * tm + row) % dseq)
        last = jnp.where(i < nct, seq - 1, dseq - 1)
        z_prev = jnp.where(pos == 0, 0.0, zs_ref[h8 - 1:h8 - 1 + tm, :])
        z_next = jnp.where(pos == last, 0.0, zs_ref[h8 + 1:h8 + 1 + tm, :])
        cv = cw_ref[0:1, :] * z_prev + cw_ref[1:2, :] * z_ref[...] + cw_ref[2:3, :] * z_next
        yb = gb_ref[...] * cv
        y = y + _dot(yb.astype(BF16), w_ref[ka:, :])
    o_ref[...] = x_ref[...] + mod_ref[2:3, :] * y


def _outproj(x, mods, ya, w_out, conv_in=None, *, nc, seq, dseq, tm):
    n, d = x.shape
    ka = ya.shape[1]
    nct = nc // tm
    per = dseq // tm
    conv = conv_in is not None
    kern = functools.partial(_outproj_kernel, conv=conv, nct=nct, seq=seq, dseq=dseq)
    in_specs = [
        pl.BlockSpec((tm, d), lambda i: (i, 0)),
        pl.BlockSpec((None, 6, d), lambda i: (_group(i, nct, per), 0, 0)),
        pl.BlockSpec((tm, ka), lambda i: (i, 0)),
    ]
    args = [x, mods, ya]
    scratch = []
    if conv:
        gb, z, cw = conv_in
        nb = z.shape[1]
        t8 = tm // SUBLANES
        n8 = n // SUBLANES
        in_specs += [
            pl.BlockSpec((tm, nb), lambda i: (i, 0)),
            pl.BlockSpec((tm, nb), lambda i: (i, 0)),
            pl.BlockSpec((SUBLANES, nb), lambda i: (jnp.maximum(i * t8 - 1, 0), 0)),
            pl.BlockSpec((SUBLANES, nb), lambda i: (jnp.minimum((i + 1) * t8, n8 - 1), 0)),
            pl.BlockSpec((3, nb), lambda i: (0, 0)),
        ]
        args += [gb, z, z, z, cw]
        scratch = [pltpu.VMEM((tm + 2 * SUBLANES, nb), F32)]
    in_specs.append(pl.BlockSpec(w_out.shape, lambda i: (0, 0), pipeline_mode=pl.Buffered(1)))
    args.append(w_out)
    return pl.pallas_call(
        kern,
        out_shape=jax.ShapeDtypeStruct((n, d), F32),
        grid=(n // tm,),
        in_specs=in_specs,
        out_specs=pl.BlockSpec((tm, d), lambda i: (i, 0)),
        scratch_shapes=scratch,
        compiler_params=_params("arbitrary"),
        name="outproj_conv" if conv else "outproj",
    )(*args)


def _ffn_kernel(x_ref, mod_ref, g2_ref, wg_ref, wu_ref, wo_ref, o_ref, h_ref):
    f = pl.program_id(1)

    @pl.when(f == 0)
    def _():
        h = _modulated(x_ref[...], g2_ref[...], mod_ref[3:4, :], mod_ref[4:5, :])
        h_ref[...] = h.astype(BF16)
        o_ref[...] = jnp.zeros_like(o_ref)

    h = h_ref[...]
    g = _dot(h, wg_ref[...])
    u = _dot(h, wu_ref[...])
    a = (g / (1.0 + jnp.exp(-g))) * u
    o_ref[...] += _dot(a.astype(BF16), wo_ref[...])

    @pl.when(f == pl.num_programs(1) - 1)
    def _():
        o_ref[...] = x_ref[...] + mod_ref[5:6, :] * o_ref[...]


def _ffn(x, mods, g2, w_in, w_out, *, nc, dseq, tm, tf):
    n, d = x.shape
    dff = w_out.shape[0]
    nf = dff // tf
    nct = nc // tm
    per = dseq // tm
    return pl.pallas_call(
        _ffn_kernel,
        out_shape=jax.ShapeDtypeStruct((n, d), F32),
        grid=(n // tm, nf),
        in_specs=[
            pl.BlockSpec((tm, d), lambda i, f: (i, 0)),
            pl.BlockSpec((None, 6, d), lambda i, f: (_group(i, nct, per), 0, 0)),
            pl.BlockSpec((1, d), lambda i, f: (0, 0)),
            pl.BlockSpec((d, tf), lambda i, f: (0, f)),
            pl.BlockSpec((d, tf), lambda i, f: (0, nf + f)),
            pl.BlockSpec((tf, d), lambda i, f: (f, 0)),
        ],
        out_specs=pl.BlockSpec((tm, d), lambda i, f: (i, 0)),
        scratch_shapes=[pltpu.VMEM((tm, d), BF16)],
        compiler_params=_params("arbitrary", "arbitrary"),
        name="ffn",
    )(x, mods, g2, w_in, w_in, w_out)


def _rope(x, c, s):
    lane = lax.broadcasted_iota(jnp.int32, x.shape, 1)
    quarter = ROPE // 4
    swapped = jnp.where(lane % (2 * quarter) < quarter,
                        pltpu.roll(x, LANES - quarter, 1), pltpu.roll(x, quarter, 1))
    return x * c + swapped * s


def _mla_kv_expand(ckv_bf16, krp, wkv_ref, gkn_ref, gkr_ref, c, s, k_ref, v_ref):
    kv = _dot(ckv_bf16, wkv_ref[...])
    krr = _rope(krp * gkr_ref[...], c, s)
    ss_kr = jnp.sum(krp * krp, axis=-1, keepdims=True)
    for hh in range(H_C):
        base = hh * (NOPE + V_DIM)
        kn = kv[:, base:base + NOPE]
        rstd = lax.rsqrt((jnp.sum(kn * kn, axis=-1, keepdims=True) + ss_kr) / QK_DIM + EPS)
        k_ref[:, hh * QK_PAD:hh * QK_PAD + NOPE] = (kn * rstd * gkn_ref[...]).astype(BF16)
        k_ref[:, hh * QK_PAD + NOPE:(hh + 1) * QK_PAD] = (krr * rstd).astype(BF16)
        v_ref[:, hh * V_DIM:(hh + 1) * V_DIM] = kv[:, base + NOPE:base + NOPE + V_DIM].astype(BF16)


def _inproj_c_kernel(x_ref, mod_ref, g1_ref, wd_ref, gcq_ref, gckv_ref, wq_ref, gq_ref,
                     wkv_ref, gkn_ref, gkr_ref, c_ref, s_ref,
                     q_ref, k_ref, v_ref, ckv_ref, kr_ref, *, nct):
    i = pl.program_id(0)
    h = _modulated(x_ref[...], g1_ref[...], mod_ref[0:1, :], mod_ref[1:2, :]).astype(BF16)
    dn = _dot(h, wd_ref[...])
    cq = dn[:, :Q_LORA]
    ckv = dn[:, Q_LORA:Q_LORA + KV_LORA]
    krp = dn[:, Q_LORA + KV_LORA:]
    cqn = cq * lax.rsqrt(jnp.mean(cq * cq, axis=-1, keepdims=True) + EPS) * gcq_ref[...]
    ckvn = ckv * lax.rsqrt(jnp.mean(ckv * ckv, axis=-1, keepdims=True) + EPS) * gckv_ref[...]

    @pl.when(i < nct)
    def _():
        ckv_ref[...] = ckvn
        kr_ref[...] = krp[:, :ROPE]

    c = c_ref[...]
    s = s_ref[...]
    q = _dot(cqn.astype(BF16), wq_ref[...])
    for hh in range(H_C):
        qh = q[:, hh * QK_PAD:(hh + 1) * QK_PAD]
        rstd = lax.rsqrt(jnp.sum(qh * qh, axis=-1, keepdims=True) / QK_DIM + EPS)
        qn = qh * rstd * gq_ref[...]
        q_ref[:, hh * QK_PAD:hh * QK_PAD + NOPE] = qn[:, :NOPE].astype(BF16)
        q_ref[:, hh * QK_PAD + NOPE:(hh + 1) * QK_PAD] = _rope(qn[:, NOPE:], c, s).astype(BF16)
    _mla_kv_expand(ckvn.astype(BF16), krp, wkv_ref, gkn_ref, gkr_ref, c, s, k_ref, v_ref)


def _inproj_c(x, mods, g1, wd, gcq, gckv, wq, gq, wkv, gkn, gkr, rope_c, rope_s,
              *, nc, dseq, tm):
    n, d = x.shape
    nct = nc // tm
    per = dseq // tm
    kern = functools.partial(_inproj_c_kernel, nct=nct)
    const = lambda shape: pl.BlockSpec(shape, lambda i: (0,) * len(shape),
                                       pipeline_mode=pl.Buffered(1))
    rope_idx = lambda i: (jnp.where(i < nct, 0, 1 + (i - nct) % per), 0)
    ctx_idx = lambda i: (jnp.minimum(i, nct - 1), 0)
    return pl.pallas_call(
        kern,
        out_shape=(jax.ShapeDtypeStruct((n, H_C * QK_PAD), BF16),
                   jax.ShapeDtypeStruct((n, H_C * QK_PAD), BF16),
                   jax.ShapeDtypeStruct((n, H_C * V_DIM), BF16),
                   jax.ShapeDtypeStruct((nc, KV_LORA), F32),
                   jax.ShapeDtypeStruct((nc, ROPE), F32)),
        grid=(n // tm,),
        in_specs=[
            pl.BlockSpec((tm, d), lambda i: (i, 0)),
            pl.BlockSpec((None, 6, d), lambda i: (_group(i, nct, per), 0, 0)),
            const((1, d)), const(wd.shape), const((1, Q_LORA)), const((1, KV_LORA)),
            const(wq.shape), const((1, QK_PAD)), const(wkv.shape),
            const((1, NOPE)), const((1, LANES)),
            pl.BlockSpec((tm, LANES), rope_idx),
            pl.BlockSpec((tm, LANES), rope_idx),
        ],
        out_specs=(
            pl.BlockSpec((tm, H_C * QK_PAD), lambda i: (i, 0)),
            pl.BlockSpec((tm, H_C * QK_PAD), lambda i: (i, 0)),
            pl.BlockSpec((tm, H_C * V_DIM), lambda i: (i, 0)),
            pl.BlockSpec((tm, KV_LORA), ctx_idx),
            pl.BlockSpec((tm, ROPE), ctx_idx),
        ),
        compiler_params=_params("arbitrary"),
        name="inproj_c",
    )(x, mods, g1, wd, gcq, gckv, wq, gq, wkv, gkn, gkr, rope_c, rope_s)


def _cache_kv_kernel(ckv_ref, krp_ref, wkv_ref, gkn_ref, gkr_ref, c_ref, s_ref, k_ref, v_ref):
    _mla_kv_expand(ckv_ref[...].astype(BF16), krp_ref[...], wkv_ref, gkn_ref, gkr_ref,
                   c_ref[...], s_ref[...], k_ref, v_ref)


def _cache_kv(ckv, krp, wkv, gkn, gkr, rope_c, rope_s, *, tm):
    n = ckv.shape[0]
    const = lambda shape: pl.BlockSpec(shape, lambda i: (0,) * len(shape))
    return pl.pallas_call(
        _cache_kv_kernel,
        out_shape=(jax.ShapeDtypeStruct((n, H_C * QK_PAD), BF16),
                   jax.ShapeDtypeStruct((n, H_C * V_DIM), BF16)),
        grid=(n // tm,),
        in_specs=[
            pl.BlockSpec((tm, KV_LORA), lambda i: (i, 0)),
            pl.BlockSpec((tm, LANES), lambda i: (i, 0)),
            const(wkv.shape), const((1, NOPE)), const((1, LANES)),
            const((tm, LANES)), const((tm, LANES)),
        ],
        out_specs=(pl.BlockSpec((tm, H_C * QK_PAD), lambda i: (i, 0)),
                   pl.BlockSpec((tm, H_C * V_DIM), lambda i: (i, 0))),
        compiler_params=_params("arbitrary"),
        name="cache_kv",
    )(ckv, krp, wkv, gkn, gkr, rope_c, rope_s)


def _mla_attn_kernel(q_ref, k_ref, v_ref, kx_ref, vx_ref, ya_ref, o_ref, *, tq, scale):
    del ya_ref
    def body(t, carry):
        r0 = pl.multiple_of(t * tq, tq)
        q = q_ref[pl.ds(r0, tq), :]
        s_x = _dot_nt(q, kx_ref[...]) * scale
        s_l = _dot_nt(q, k_ref[...]) * scale
        m = jnp.maximum(jnp.max(s_x, axis=-1, keepdims=True),
                        jnp.max(s_l, axis=-1, keepdims=True))
        p_x = jnp.exp(s_x - m)
        p_l = jnp.exp(s_l - m)
        l = jnp.sum(p_x, axis=-1, keepdims=True) + jnp.sum(p_l, axis=-1, keepdims=True)
        o = (_dot(p_x.astype(BF16), vx_ref[...]) + _dot(p_l.astype(BF16), v_ref[...])) / l
        o_ref[pl.ds(r0, tq), :] = o.astype(BF16)
        return carry

    lax.fori_loop(0, q_ref.shape[0] // tq, body, 0)


def _mla_attn(q, k, v, kx, vx, ya, *, nc, dseq, dbatch, past, tq):
    off = nc // dseq
    kern = functools.partial(_mla_attn_kernel, tq=tq, scale=QK_DIM ** -0.5)
    return pl.pallas_call(
        kern,
        out_shape=jax.ShapeDtypeStruct(ya.shape, ya.dtype),
        grid=(dbatch, H_C),
        in_specs=[
            pl.BlockSpec((dseq, QK_PAD), lambda b, h: (off + b, h)),
            pl.BlockSpec((dseq, QK_PAD), lambda b, h: (off + b, h)),
            pl.BlockSpec((dseq, V_DIM), lambda b, h: (off + b, h)),
            pl.BlockSpec((past, QK_PAD), lambda b, h: (b, h)),
            pl.BlockSpec((past, V_DIM), lambda b, h: (b, h)),
            pl.BlockSpec(memory_space=pl.ANY),
        ],
        out_specs=pl.BlockSpec((dseq, V_DIM), lambda b, h: (off + b, h)),
        input_output_aliases={5: 0},
        compiler_params=_params("arbitrary", "arbitrary"),
        name="mla_attn",
    )(q, k, v, kx, vx, ya)


def _rope_tables(dseq, tm):
    t = np.arange(dseq)
    quarter = ROPE // 4
    freqs = jnp.asarray(ROPE_BASE, F32) ** (-jnp.arange(quarter, dtype=F32) / quarter)
    ang_r = jnp.asarray(t // GRID_W, F32)[:, None] * freqs
    ang_c = jnp.asarray(t % GRID_W, F32)[:, None] * freqs
    zeros = jnp.zeros((dseq, LANES - ROPE), F32)
    cos = jnp.concatenate([jnp.cos(ang_r), jnp.cos(ang_r), jnp.cos(ang_c), jnp.cos(ang_c), zeros], axis=1)
    sin = jnp.concatenate([-jnp.sin(ang_r), jnp.sin(ang_r), -jnp.sin(ang_c), jnp.sin(ang_c), zeros], axis=1)
    ident = jnp.concatenate([jnp.ones((tm, ROPE), F32), jnp.zeros((tm, LANES - ROPE), F32)], axis=1)
    return (jnp.concatenate([ident, cos], axis=0),
            jnp.concatenate([jnp.zeros((tm, LANES), F32), sin], axis=0))


def _pick_tile(pref, *lengths):
    t = pref
    while any(l % t for l in lengths):
        t //= 2
    return t


def kernel(x_prompt, x_sample, cache_nat_k, cache_nat_v, cache_mla_ckv, cache_mla_krope, c, c_ctx,
           norm1_g, norm2_g, w_ada, b_ada, w_in_ab, g_qn_a, g_kn_a, rpb_a, conv_b_w, w_out_ab,
           w_down_c, g_cq, g_ckv, w_uq_c, w_ukv_c, g_qn_c, g_kn_c, w_o_c, w_ffn_in, w_ffn_out):
    batch, seq, d = x_prompt.shape
    dbatch, dseq, _ = x_sample.shape
    depth = w_ada.shape[0]
    past = cache_nat_k.shape[3]
    nc, ns = batch * seq, dbatch * dseq
    n = nc + ns
    dff = w_ffn_out.shape[1]
    assert nc % dseq == 0 and dseq % seq == 0 and dseq % GRID_W == 0

    tm = _pick_tile(512, nc, dseq)
    tm_c = _pick_tile(256, nc, dseq)
    tf = _pick_tile(512, dff)
    assert tm % seq == 0

    x = jnp.concatenate([x_prompt.reshape(nc, d), x_sample.reshape(ns, d)], axis=0)

    groups = 1 + dbatch
    gp = -(-groups // SUBLANES) * SUBLANES
    cond = jnp.concatenate([c_ctx[None, :], c, jnp.zeros((gp - groups, d), F32)], axis=0)
    mods_all = _adaln(cond, w_ada, b_ada).reshape(depth, gp, 6, d)

    rope_c, rope_s = _rope_tables(dseq, tm_c)
    ident_c = jnp.concatenate([jnp.ones((past, ROPE), F32), jnp.zeros((past, LANES - ROPE), F32)], axis=1)
    shape_kw = dict(nc=nc, dseq=dseq)

    nat_k, nat_v, mla_ckv, mla_kr = [], [], [], []
    for l in range(depth):
        i = l // 2
        mods = mods_all[l]
        if l % 2 == 0:
            qkv, gb, z, nk, nv = _inproj_ab(
                x, mods, norm1_g[l][None, :], w_in_ab[i].astype(BF16),
                g_qn_a[i][None, :], g_kn_a[i][None, :],
                seq=seq, batch=batch, tm=tm, **shape_kw)
            nat_k.append(nk)
            nat_v.append(nv)
            ya = _ctx_attn(qkv, qkv, qkv, (0, 1, 2), n=n, batch=batch, seq=seq,
                           heads=H_A, dq=DH_A, dv=DH_A, scale=DH_A ** -0.5)
            ya = _nat_attn(rpb_a[i], qkv, cache_nat_k, cache_nat_v, ya, i,
                           dbatch=dbatch, **shape_kw)
            x = _outproj(x, mods, ya, w_out_ab[i].astype(BF16), (gb, z, conv_b_w[i]),
                         seq=seq, tm=tm, **shape_kw)
        else:
            wd = jnp.pad(w_down_c[i], ((0, 0), (0, LANES - ROPE))).astype(BF16)
            wq = jnp.pad(w_uq_c[i].reshape(Q_LORA, H_C, QK_DIM),
                         ((0, 0), (0, 0), (0, QK_PAD - QK_DIM))).reshape(Q_LORA, H_C * QK_PAD).astype(BF16)
            wkv = w_ukv_c[i].astype(BF16)
            gq = jnp.pad(g_qn_c[i], (0, QK_PAD - QK_DIM))[None, :]
            gkn = g_kn_c[i][None, :NOPE]
            gkr = jnp.pad(g_kn_c[i][NOPE:], (0, LANES - ROPE))[None, :]
            q, k, v, ckv_n, kr = _inproj_c(
                x, mods, norm1_g[l][None, :], wd, g_cq[i][None, :], g_ckv[i][None, :],
                wq, gq, wkv, gkn, gkr, rope_c, rope_s, tm=tm_c, **shape_kw)
            mla_ckv.append(ckv_n.reshape(batch, seq, KV_LORA))
            mla_kr.append(kr.reshape(batch, seq, ROPE))
            kx, vx = _cache_kv(
                cache_mla_ckv[:, i].reshape(dbatch * past, KV_LORA),
                jnp.pad(cache_mla_krope[:, i].reshape(dbatch * past, ROPE), ((0, 0), (0, LANES - ROPE))),
                wkv, gkn, gkr, ident_c, jnp.zeros((past, LANES), F32), tm=past)
            ya = _ctx_attn(q, k, v, (0, 0, 0), n=n, batch=batch, seq=seq,
                           heads=H_C, dq=QK_PAD, dv=V_DIM, scale=QK_DIM ** -0.5)
            ya = _mla_attn(q, k, v, kx, vx, ya, dbatch=dbatch, past=past,
                           tq=_pick_tile(256, dseq), **shape_kw)
            x = _outproj(x, mods, ya, w_o_c[i].astype(BF16), seq=seq, tm=tm, **shape_kw)
        x = _ffn(x, mods, norm2_g[l][None, :], w_ffn_in[l].astype(BF16), w_ffn_out[l].astype(BF16),
                 tm=tm, tf=tf, **shape_kw)

    return (x[:nc].reshape(batch, seq, d), x[nc:].reshape(dbatch, dseq, d),
            jnp.stack(nat_k, axis=1), jnp.stack(nat_v, axis=1),
            jnp.stack(mla_ckv, axis=1), jnp.stack(mla_kr, axis=1))
```

```python
import functools

import numpy as np
import jax
import jax.numpy as jnp
from jax import lax
from jax.experimental import pallas as pl
from jax.experimental.pallas import tpu as pltpu

F32 = jnp.float32
BF16 = jnp.bfloat16

EPS = 1e-6
GRID_W = 64
WIN_R = 8
WIN_C = 16
ROPE_BASE = 10000.0
H_A = 8
DH_A = 128
H_C = 16
Q_LORA = 512
KV_LORA = 256
NOPE = 128
ROPE = 64
V_DIM = 128
QK_DIM = NOPE + ROPE
QK_PAD = 256
LANES = 128
SUBLANES = 8
NEG = -1e30
LOG2E = 1.4426950408889634

VMEM_LIMIT = 56 * 1024 * 1024


def _params(*sem):
    return pltpu.CompilerParams(dimension_semantics=sem, vmem_limit_bytes=VMEM_LIMIT)


def _group(i, nct, per):
    return jnp.where(i < nct, 0, 1 + (i - nct) // per)


def _modulated(x, g, shift, scale):
    ms = jnp.mean(x * x, axis=-1, keepdims=True)
    return (x * lax.rsqrt(ms + EPS) * g) * (1.0 + scale) + shift


def _dot(a, b):
    return jnp.dot(a, b, preferred_element_type=F32)


def _dot_nt(a, b):
    return lax.dot_general(a, b, (((1,), (1,)), ((), ())), preferred_element_type=F32)


def _ada_kernel(c_ref, w_ref, b_ref, o_ref):
    c = c_ref[...]
    s = (c / (1.0 + jnp.exp(-c))).astype(BF16)
    o_ref[...] = _dot(s, w_ref[...].astype(BF16)) + b_ref[...]


def _adaln(cond, w_ada, b_ada):
    depth, d, n6 = w_ada.shape
    gp = cond.shape[0]
    tn = 1024 if n6 % 1024 == 0 else n6
    return pl.pallas_call(
        _ada_kernel,
        out_shape=jax.ShapeDtypeStruct((depth, gp, n6), F32),
        grid=(depth, n6 // tn),
        in_specs=[
            pl.BlockSpec((gp, d), lambda l, n: (0, 0)),
            pl.BlockSpec((None, d, tn), lambda l, n: (l, 0, n)),
            pl.BlockSpec((None, 1, tn), lambda l, n: (l, 0, n)),
        ],
        out_specs=pl.BlockSpec((None, gp, tn), lambda l, n: (l, 0, n)),
        compiler_params=_params("arbitrary", "arbitrary"),
        name="adaln",
    )(cond, w_ada, b_ada.reshape(depth, 1, n6))


def _inproj_ab_kernel(x_ref, mod_ref, g1_ref, w_ref, gq_ref, gk_ref, *rest, nct, seq, first):
    if first:
        qkv_ref, gb_ref, z_ref, natk_ref, natv_ref, h_ref, gc_ref = rest
    else:
        _, _, qkv_ref, gb_ref, z_ref, natk_ref, natv_ref, h_ref, gc_ref = rest
    i = pl.program_id(0)
    n = pl.program_id(1)
    tm = x_ref.shape[0]

    def proj():
        return _dot(h_ref[...], w_ref[...])

    def nat_store(nat_ref, bb, hh, val):
        if first:
            nat_ref[bb, 0, hh] = val
        else:
            nat_ref[bb, hh] = val

    def head_norm(r, gain_ref, nat_ref):
        for hh in range(H_A):
            t = r[:, hh * DH_A:(hh + 1) * DH_A]
            ms = jnp.mean(t * t, axis=-1, keepdims=True)
            val = t * lax.rsqrt(ms + EPS) * gain_ref[...]
            qkv_ref[:, hh * DH_A:(hh + 1) * DH_A] = val.astype(BF16)
            if nat_ref is not None:
                @pl.when(i < nct)
                def _():
                    for bb in range(tm // seq):
                        nat_store(nat_ref, bb, hh, val[bb * seq:(bb + 1) * seq, :])

    @pl.when(n == 0)
    def _():
        h = _modulated(x_ref[...], g1_ref[...], mod_ref[0:1, :], mod_ref[1:2, :])
        h_ref[...] = h.astype(BF16)
        if first and natk_ref.shape[1] > 1:
            @pl.when(i < nct)
            def _():
                natk_ref[:, 1:] = jnp.zeros_like(natk_ref[:, 1:])
                natv_ref[:, 1:] = jnp.zeros_like(natv_ref[:, 1:])
        head_norm(proj(), gq_ref, None)

    @pl.when(n == 1)
    def _():
        head_norm(proj(), gk_ref, natk_ref)

    @pl.when(n == 2)
    def _():
        r = proj()
        qkv_ref[...] = r.astype(BF16)

        @pl.when(i < nct)
        def _():
            for bb in range(tm // seq):
                for hh in range(H_A):
                    nat_store(natv_ref, bb, hh, r[bb * seq:(bb + 1) * seq, hh * DH_A:(hh + 1) * DH_A])

    @pl.when(n == 3)
    def _():
        gb_ref[...] = proj()

    @pl.when(n == 4)
    def _():
        gc_ref[...] = proj()

    @pl.when(n == 5)
    def _():
        z_ref[...] = gc_ref[...] * proj()


def _inproj_ab(x, mods, g1, w_in, gq, gk, nat_prev, slot, *, nc, seq, dseq, batch, n_even, tm):
    n, d = x.shape
    na = H_A * DH_A
    nct = nc // tm
    per = dseq // tm
    first = nat_prev is None
    kern = functools.partial(_inproj_ab_kernel, nct=nct, seq=seq, first=first)
    nat_shape = jax.ShapeDtypeStruct((batch, n_even, H_A, seq, DH_A), F32)
    if first:
        nat_spec = pl.BlockSpec((tm // seq, n_even, H_A, seq, DH_A),
                                lambda i, j: (jnp.minimum(i, nct - 1), 0, 0, 0, 0))
    else:
        nat_spec = pl.BlockSpec((tm // seq, None, H_A, seq, DH_A),
                                lambda i, j: (jnp.minimum(i, nct - 1), slot, 0, 0, 0))
    in_specs = [
        pl.BlockSpec((tm, d), lambda i, j: (i, 0)),
        pl.BlockSpec((None, 6, d), lambda i, j: (_group(i, nct, per), 0, 0)),
        pl.BlockSpec((1, d), lambda i, j: (0, 0)),
        pl.BlockSpec((None, d, na), lambda i, j: (slot, 0, j)),
        pl.BlockSpec((1, DH_A), lambda i, j: (0, 0)),
        pl.BlockSpec((1, DH_A), lambda i, j: (0, 0)),
    ]
    args = [x, mods, g1, w_in, gq, gk]
    aliases = {}
    if not first:
        in_specs += [pl.BlockSpec(memory_space=pl.ANY)] * 2
        args += list(nat_prev)
        aliases = {6: 3, 7: 4}
    return pl.pallas_call(
        kern,
        out_shape=(jax.ShapeDtypeStruct((n, 3 * na), BF16),
                   jax.ShapeDtypeStruct((n, na), F32),
                   jax.ShapeDtypeStruct((n, na), F32),
                   nat_shape, nat_shape),
        grid=(n // tm, 6),
        in_specs=in_specs,
        out_specs=(
            pl.BlockSpec((tm, na), lambda i, j: (i, jnp.minimum(j, 2))),
            pl.BlockSpec((tm, na), lambda i, j: (i, 0)),
            pl.BlockSpec((tm, na), lambda i, j: (i, 0)),
            nat_spec, nat_spec,
        ),
        scratch_shapes=[pltpu.VMEM((tm, d), BF16), pltpu.VMEM((tm, na), F32)],
        input_output_aliases=aliases,
        compiler_params=_params("arbitrary", "arbitrary"),
        name="inproj_ab",
    )(*args)


def _ctx_attn_kernel(q_ref, k_ref, v_ref, o_ref, *, heads, dq, dv):
    def scores(hh):
        return _dot_nt(q_ref[:, hh * dq:(hh + 1) * dq], k_ref[:, hh * dq:(hh + 1) * dq])

    s_next = scores(0)
    for hh in range(heads):
        s = s_next
        if hh + 1 < heads:
            s_next = scores(hh + 1)
        m = jnp.max(s, axis=-1, keepdims=True)
        p = jnp.exp2(s - m)
        l = jnp.sum(p, axis=-1, keepdims=True)
        o = _dot(p.astype(BF16), v_ref[:, hh * dv:(hh + 1) * dv]) / l
        o_ref[:, hh * dv:(hh + 1) * dv] = o.astype(BF16)


def _ctx_attn(q_arr, k_arr, v_arr, cols, *, nc, batch, seq, heads, dq, dv):
    cq, ck, cv = cols
    kern = functools.partial(_ctx_attn_kernel, heads=heads, dq=dq, dv=dv)
    return pl.pallas_call(
        kern,
        out_shape=jax.ShapeDtypeStruct((nc, heads * dv), BF16),
        grid=(batch,),
        in_specs=[
            pl.BlockSpec((seq, heads * dq), lambda b: (b, cq)),
            pl.BlockSpec((seq, heads * dq), lambda b: (b, ck)),
            pl.BlockSpec((seq, heads * dv), lambda b: (b, cv)),
        ],
        out_specs=pl.BlockSpec((seq, heads * dv), lambda b: (b, 0)),
        compiler_params=_params("arbitrary"),
        name="ctx_attn",
    )(q_arr, k_arr, v_arr)


NAT_QR = 8
NAT_KR = 16


def _nat_window(a, rows):
    return int(np.clip(a * NAT_QR - WIN_R // 2, 0, rows - NAT_KR))


def _nat_attn_kernel(rpb_ref, q_ref, k_ref, v_ref, ck_ref, cv_ref, o_ref,
                     t2_ref, bias_ref, *, rows):
    hh = pl.program_id(0)
    b = pl.program_id(1)
    nblk = rows // NAT_QR
    n_dr = 2 * WIN_R - 1
    n_dc = 2 * WIN_C - 1

    @pl.when(b == 0)
    def _():
        j = lax.broadcasted_iota(jnp.int32, (GRID_W, LANES), 0)
        kc = lax.broadcasted_iota(jnp.int32, (GRID_W, LANES), 1) % GRID_W
        cs = jnp.clip(j - WIN_C // 2, 0, GRID_W - WIN_C)
        col_in = (kc >= cs) & (kc < cs + WIN_C)
        dc = kc - j + (WIN_C - 1)
        for d in range(n_dr):
            t = jnp.full((GRID_W, LANES), NEG, F32)
            for c in range(n_dc):
                t = jnp.where(col_in & (dc == c), rpb_ref[hh, d * n_dc + c] * LOG2E, t)
            t2_ref[d] = t
        lane = lax.broadcasted_iota(jnp.int32, (GRID_W, LANES), 1)
        negt = jnp.full((GRID_W, LANES), NEG, F32)
        for a in range(nblk):
            ws = _nat_window(a, rows)
            for qr in range(NAT_QR):
                r = a * NAT_QR + qr
                rs = int(np.clip(r - WIN_R // 2, 0, rows - WIN_R))
                for pp in range(NAT_KR // 2):
                    halves = []
                    for kk in (2 * pp, 2 * pp + 1):
                        kr = ws + kk
                        halves.append(t2_ref[kr - r + WIN_R - 1] if rs <= kr < rs + WIN_R else None)
                    if halves[0] is None and halves[1] is None:
                        blk = negt
                    else:
                        left = negt if halves[0] is None else halves[0]
                        right = negt if halves[1] is None else halves[1]
                        blk = jnp.where(lane < GRID_W, left, right)
                    bias_ref[a, qr * GRID_W:(qr + 1) * GRID_W, pp * LANES:(pp + 1) * LANES] = blk

    ck = ck_ref[...].astype(BF16)
    cv = cv_ref[...].astype(BF16)
    tq = NAT_QR * GRID_W
    tk = NAT_KR * GRID_W

    def scores(a):
        ws = _nat_window(a, rows)
        q = q_ref[a * tq:(a + 1) * tq, :]
        kw = k_ref[ws * GRID_W:ws * GRID_W + tk, :]
        return _dot_nt(q, kw) + bias_ref[a], _dot_nt(q, ck)

    s_next = scores(0)
    for a in range(nblk):
        s_loc, s_ctx = s_next
        if a + 1 < nblk:
            s_next = scores(a + 1)
        ws = _nat_window(a, rows)
        vw = v_ref[ws * GRID_W:ws * GRID_W + tk, :]
        m = jnp.maximum(jnp.max(s_loc, axis=-1, keepdims=True),
                        jnp.max(s_ctx, axis=-1, keepdims=True))
        p_loc = jnp.exp2(s_loc - m)
        p_ctx = jnp.exp2(s_ctx - m)
        l = jnp.sum(p_loc, axis=-1, keepdims=True) + jnp.sum(p_ctx, axis=-1, keepdims=True)
        o = (_dot(p_loc.astype(BF16), vw) + _dot(p_ctx.astype(BF16), cv)) / l
        o_ref[a * tq:(a + 1) * tq, :] = o.astype(BF16)


def _nat_attn(rpb, qkv, cache_k, cache_v, slot, *, nc, dseq, dbatch):
    rows = dseq // GRID_W
    assert rows % NAT_QR == 0 and rows >= NAT_KR
    nblk = rows // NAT_QR
    past = cache_k.shape[3]
    off = nc // dseq
    n_dr, n_dc = 2 * WIN_R - 1, 2 * WIN_C - 1
    kern = functools.partial(_nat_attn_kernel, rows=rows)
    cache_spec = pl.BlockSpec((None, None, None, past, DH_A), lambda h, b: (b, slot, h, 0, 0))
    return pl.pallas_call(
        kern,
        out_shape=jax.ShapeDtypeStruct((dbatch * dseq, H_A * DH_A), BF16),
        grid=(H_A, dbatch),
        in_specs=[
            pl.BlockSpec(memory_space=pltpu.SMEM),
            pl.BlockSpec((dseq, DH_A), lambda h, b: (off + b, h)),
            pl.BlockSpec((dseq, DH_A), lambda h, b: (off + b, H_A + h)),
            pl.BlockSpec((dseq, DH_A), lambda h, b: (off + b, 2 * H_A + h)),
            cache_spec, cache_spec,
        ],
        out_specs=pl.BlockSpec((dseq, DH_A), lambda h, b: (b, h)),
        scratch_shapes=[pltpu.VMEM((n_dr, GRID_W, LANES), F32),
                        pltpu.VMEM((nblk, NAT_QR * GRID_W, NAT_KR * GRID_W), F32)],
        compiler_params=_params("arbitrary", "arbitrary"),
        name="nat_attn",
    )(rpb.reshape(H_A, n_dr * n_dc), qkv, qkv, qkv, cache_k, cache_v)


def _outproj_kernel(*refs, conv, nct, seq, dseq):
    if conv:
        (x_ref, mod_ref, yac_ref, yas_ref, gb_ref, z_ref, zp_ref, zn_ref, cw_ref, w_ref,
         o_ref, zs_ref) = refs
    else:
        x_ref, mod_ref, yac_ref, yas_ref, w_ref, o_ref = refs
    i = pl.program_id(0)
    tm = x_ref.shape[0]
    ka = yac_ref.shape[1]
    ya = jnp.where(i < nct, yac_ref[...], yas_ref[...])
    y = _dot(ya, w_ref[0:ka, :])
    if conv:
        h8 = SUBLANES
        zs_ref[0:h8, :] = zp_ref[...]
        zs_ref[h8:h8 + tm, :] = z_ref[...]
        zs_ref[h8 + tm:2 * h8 + tm, :] = zn_ref[...]
        row = lax.broadcasted_iota(jnp.int32, (tm, 1), 0)
        pos = jnp.where(i < nct, row % seq, ((i - nct) * tm + row) % dseq)
        last = jnp.where(i < nct, seq - 1, dseq - 1)
        z_prev = jnp.where(pos == 0, 0.0, zs_ref[h8 - 1:h8 - 1 + tm, :])
        z_next = jnp.where(pos == last, 0.0, zs_ref[h8 + 1:h8 + 1 + tm, :])
        cv = cw_ref[0:1, :] * z_prev + cw_ref[1:2, :] * z_ref[...] + cw_ref[2:3, :] * z_next
        yb = gb_ref[...] * cv
        y = y + _dot(yb.astype(BF16), w_ref[ka:, :])
    o_ref[...] = x_ref[...] + mod_ref[2:3, :] * y


def _outproj(x, mods, ya_c, ya_s, w_out, slot, conv_in=None, *, nc, seq, dseq, tm):
    n, d = x.shape
    ka = ya_c.shape[1]
    nct = nc // tm
    per = dseq // tm
    conv = conv_in is not None
    kern = functools.partial(_outproj_kernel, conv=conv, nct=nct, seq=seq, dseq=dseq)
    in_specs = [
        pl.BlockSpec((tm, d), lambda i: (i, 0)),
        pl.BlockSpec((None, 6, d), lambda i: (_group(i, nct, per), 0, 0)),
        pl.BlockSpec((tm, ka), lambda i: (jnp.minimum(i, nct - 1), 0)),
        pl.BlockSpec((tm, ka), lambda i: (jnp.maximum(i - nct, 0), 0)),
    ]
    args = [x, mods, ya_c, ya_s]
    scratch = []
    if conv:
        gb, z, cw = conv_in
        nb = z.shape[1]
        t8 = tm // SUBLANES
        n8 = n // SUBLANES
        in_specs += [
            pl.BlockSpec((tm, nb), lambda i: (i, 0)),
            pl.BlockSpec((tm, nb), lambda i: (i, 0)),
            pl.BlockSpec((SUBLANES, nb), lambda i: (jnp.maximum(i * t8 - 1, 0), 0)),
            pl.BlockSpec((SUBLANES, nb), lambda i: (jnp.minimum((i + 1) * t8, n8 - 1), 0)),
            pl.BlockSpec((3, nb), lambda i: (0, 0)),
        ]
        args += [gb, z, z, z, cw]
        scratch = [pltpu.VMEM((tm + 2 * SUBLANES, nb), F32)]
    in_specs.append(pl.BlockSpec((None,) + w_out.shape[1:], lambda i: (slot, 0, 0),
                                 pipeline_mode=pl.Buffered(1)))
    args.append(w_out)
    return pl.pallas_call(
        kern,
        out_shape=jax.ShapeDtypeStruct((n, d), F32),
        grid=(n // tm,),
        in_specs=in_specs,
        out_specs=pl.BlockSpec((tm, d), lambda i: (i, 0)),
        scratch_shapes=scratch,
        compiler_params=_params("arbitrary"),
        name="outproj_conv" if conv else "outproj",
    )(*args)


def _ffn_kernel(x_ref, mod_ref, g2_ref, wg_ref, wu_ref, wo_ref, *rest, split, nct):
    if split:
        oc_ref, os_ref, h_ref, acc_ref = rest
    else:
        acc_ref, h_ref = rest
    i = pl.program_id(0)
    f = pl.program_id(1)

    @pl.when(f == 0)
    def _():
        h = _modulated(x_ref[...], g2_ref[...], mod_ref[3:4, :], mod_ref[4:5, :])
        h_ref[...] = h.astype(BF16)
        acc_ref[...] = jnp.zeros_like(acc_ref)

    h = h_ref[...]
    g = _dot(h, wg_ref[...])
    u = _dot(h, wu_ref[...])
    a = (g / (1.0 + jnp.exp(-g))) * u
    acc_ref[...] += _dot(a.astype(BF16), wo_ref[...])

    @pl.when(f == pl.num_programs(1) - 1)
    def _():
        res = x_ref[...] + mod_ref[5:6, :] * acc_ref[...]
        if split:
            @pl.when(i < nct)
            def _():
                oc_ref[...] = res

            @pl.when(i >= nct)
            def _():
                os_ref[...] = res
        else:
            acc_ref[...] = res


def _ffn(x, mods, g2, w_in, w_out, layer, *, split, nc, dseq, tm, tf):
    n, d = x.shape
    dff = w_out.shape[1]
    nf = dff // tf
    nct = nc // tm
    per = dseq // tm
    kern = functools.partial(_ffn_kernel, split=split, nct=nct)
    if split:
        out_shape = (jax.ShapeDtypeStruct((nc, d), F32), jax.ShapeDtypeStruct((n - nc, d), F32))
        out_specs = (pl.BlockSpec((tm, d), lambda i, f: (jnp.minimum(i, nct - 1), 0)),
                     pl.BlockSpec((tm, d), lambda i, f: (jnp.maximum(i - nct, 0), 0)))
        scratch = [pltpu.VMEM((tm, d), BF16), pltpu.VMEM((tm, d), F32)]
    else:
        out_shape = jax.ShapeDtypeStruct((n, d), F32)
        out_specs = pl.BlockSpec((tm, d), lambda i, f: (i, 0))
        scratch = [pltpu.VMEM((tm, d), BF16)]
    return pl.pallas_call(
        kern,
        out_shape=out_shape,
        grid=(n // tm, nf),
        in_specs=[
            pl.BlockSpec((tm, d), lambda i, f: (i, 0)),
            pl.BlockSpec((None, 6, d), lambda i, f: (_group(i, nct, per), 0, 0)),
            pl.BlockSpec((1, d), lambda i, f: (0, 0)),
            pl.BlockSpec((None, d, tf), lambda i, f: (layer, 0, f)),
            pl.BlockSpec((None, d, tf), lambda i, f: (layer, 0, nf + f)),
            pl.BlockSpec((None, tf, d), lambda i, f: (layer, f, 0)),
        ],
        out_specs=out_specs,
        scratch_shapes=scratch,
        compiler_params=_params("arbitrary", "arbitrary"),
        name="ffn",
    )(x, mods, g2, w_in, w_in, w_out)


def _rope(x, c, s):
    lane = lax.broadcasted_iota(jnp.int32, x.shape, 1)
    quarter = ROPE // 4
    swapped = jnp.where(lane % (2 * quarter) < quarter,
                        pltpu.roll(x, LANES - quarter, 1), pltpu.roll(x, quarter, 1))
    return x * c + swapped * s


def _mla_kv_expand(ckv_bf16, krp, wkv_ref, gkn_ref, gkr_ref, c, s, k_ref, v_ref):
    kv = _dot(ckv_bf16, wkv_ref[...])
    krr = _rope(krp * gkr_ref[...], c, s)
    ss_kr = jnp.sum(krp * krp, axis=-1, keepdims=True)
    for hh in range(H_C):
        base = hh * (NOPE + V_DIM)
        kn = kv[:, base:base + NOPE]
        rstd = lax.rsqrt((jnp.sum(kn * kn, axis=-1, keepdims=True) + ss_kr) / QK_DIM + EPS)
        k_ref[:, hh * QK_PAD:hh * QK_PAD + NOPE] = (kn * rstd * gkn_ref[...]).astype(BF16)
        k_ref[:, hh * QK_PAD + NOPE:(hh + 1) * QK_PAD] = (krr * rstd).astype(BF16)
        v_ref[:, hh * V_DIM:(hh + 1) * V_DIM] = kv[:, base + NOPE:base + NOPE + V_DIM].astype(BF16)


def _inproj_c_kernel(x_ref, mod_ref, g1_ref, wd_ref, gcq_ref, gckv_ref, wq_ref, gq_ref,
                     wkv_ref, gkn_ref, gkr_ref, c_ref, s_ref,
                     q_ref, k_ref, v_ref, ckv_ref, kr_ref, *, nct):
    i = pl.program_id(0)
    h = _modulated(x_ref[...], g1_ref[...], mod_ref[0:1, :], mod_ref[1:2, :]).astype(BF16)
    dn = _dot(h, wd_ref[...])
    cq = dn[:, :Q_LORA]
    ckv = dn[:, Q_LORA:Q_LORA + KV_LORA]
    krp = dn[:, Q_LORA + KV_LORA:]
    cqn = cq * lax.rsqrt(jnp.mean(cq * cq, axis=-1, keepdims=True) + EPS) * gcq_ref[...]
    ckvn = ckv * lax.rsqrt(jnp.mean(ckv * ckv, axis=-1, keepdims=True) + EPS) * gckv_ref[...]

    @pl.when(i < nct)
    def _():
        ckv_ref[...] = ckvn
        kr_ref[...] = krp[:, :ROPE]

    c = c_ref[...]
    s = s_ref[...]
    q = _dot(cqn.astype(BF16), wq_ref[...])
    for hh in range(H_C):
        qh = q[:, hh * QK_PAD:(hh + 1) * QK_PAD]
        rstd = lax.rsqrt(jnp.sum(qh * qh, axis=-1, keepdims=True) / QK_DIM + EPS)
        qn = qh * rstd * gq_ref[...]
        q_ref[:, hh * QK_PAD:hh * QK_PAD + NOPE] = qn[:, :NOPE].astype(BF16)
        q_ref[:, hh * QK_PAD + NOPE:(hh + 1) * QK_PAD] = _rope(qn[:, NOPE:], c, s).astype(BF16)
    _mla_kv_expand(ckvn.astype(BF16), krp, wkv_ref, gkn_ref, gkr_ref, c, s, k_ref, v_ref)


def _inproj_c(x, mods, g1, wd, gcq, gckv, wq, gq, wkv, gkn, gkr, rope_c, rope_s,
              *, nc, dseq, tm):
    n, d = x.shape
    nct = nc // tm
    per = dseq // tm
    kern = functools.partial(_inproj_c_kernel, nct=nct)
    const = lambda shape: pl.BlockSpec(shape, lambda i: (0,) * len(shape),
                                       pipeline_mode=pl.Buffered(1))
    rope_idx = lambda i: (jnp.where(i < nct, 0, 1 + (i - nct) % per), 0)
    ctx_idx = lambda i: (jnp.minimum(i, nct - 1), 0)
    return pl.pallas_call(
        kern,
        out_shape=(jax.ShapeDtypeStruct((n, H_C * QK_PAD), BF16),
                   jax.ShapeDtypeStruct((n, H_C * QK_PAD), BF16),
                   jax.ShapeDtypeStruct((n, H_C * V_DIM), BF16),
                   jax.ShapeDtypeStruct((nc, KV_LORA), F32),
                   jax.ShapeDtypeStruct((nc, ROPE), F32)),
        grid=(n // tm,),
        in_specs=[
            pl.BlockSpec((tm, d), lambda i: (i, 0)),
            pl.BlockSpec((None, 6, d), lambda i: (_group(i, nct, per), 0, 0)),
            const((1, d)), const(wd.shape), const((1, Q_LORA)), const((1, KV_LORA)),
            const(wq.shape), const((1, QK_PAD)), const(wkv.shape),
            const((1, NOPE)), const((1, LANES)),
            pl.BlockSpec((tm, LANES), rope_idx),
            pl.BlockSpec((tm, LANES), rope_idx),
        ],
        out_specs=(
            pl.BlockSpec((tm, H_C * QK_PAD), lambda i: (i, 0)),
            pl.BlockSpec((tm, H_C * QK_PAD), lambda i: (i, 0)),
            pl.BlockSpec((tm, H_C * V_DIM), lambda i: (i, 0)),
            pl.BlockSpec((tm, KV_LORA), ctx_idx),
            pl.BlockSpec((tm, ROPE), ctx_idx),
        ),
        compiler_params=_params("arbitrary"),
        name="inproj_c",
    )(x, mods, g1, wd, gcq, gckv, wq, gq, wkv, gkn, gkr, rope_c, rope_s)


def _cache_kv_kernel(ckv_ref, krp_ref, wkv_ref, gkn_ref, gkr_ref, c_ref, s_ref, k_ref, v_ref):
    _mla_kv_expand(ckv_ref[...].astype(BF16), krp_ref[...], wkv_ref, gkn_ref, gkr_ref,
                   c_ref[...], s_ref[...], k_ref, v_ref)


def _cache_kv(ckv, krp, wkv, gkn, gkr, rope_c, rope_s, *, tm):
    n = ckv.shape[0]
    const = lambda shape: pl.BlockSpec(shape, lambda i: (0,) * len(shape))
    return pl.pallas_call(
        _cache_kv_kernel,
        out_shape=(jax.ShapeDtypeStruct((n, H_C * QK_PAD), BF16),
                   jax.ShapeDtypeStruct((n, H_C * V_DIM), BF16)),
        grid=(n // tm,),
        in_specs=[
            pl.BlockSpec((tm, KV_LORA), lambda i: (i, 0)),
            pl.BlockSpec((tm, LANES), lambda i: (i, 0)),
            const(wkv.shape), const((1, NOPE)), const((1, LANES)),
            const((tm, LANES)), const((tm, LANES)),
        ],
        out_specs=(pl.BlockSpec((tm, H_C * QK_PAD), lambda i: (i, 0)),
                   pl.BlockSpec((tm, H_C * V_DIM), lambda i: (i, 0))),
        compiler_params=_params("arbitrary"),
        name="cache_kv",
    )(ckv, krp, wkv, gkn, gkr, rope_c, rope_s)


def _mla_attn_kernel(q_ref, k_ref, v_ref, kx_ref, vx_ref, o_ref, kall_ref, vt_ref, *, tq):
    past = kx_ref.shape[0]
    kall_ref[0:past, :] = kx_ref[...]
    kall_ref[past:, :] = k_ref[...]
    vt_ref[:, 0:past] = vx_ref[...].astype(F32).T.astype(BF16)
    vt_ref[:, past:] = v_ref[...].astype(F32).T.astype(BF16)

    def scores(t):
        q = q_ref[t * tq:(t + 1) * tq, :]
        return _dot_nt(kall_ref[...], q)

    nblk = q_ref.shape[0] // tq
    s_next = scores(0)
    for t in range(nblk):
        s = s_next
        if t + 1 < nblk:
            s_next = scores(t + 1)
        m = jnp.max(s, axis=0, keepdims=True)
        p = jnp.exp2(s - m)
        l = jnp.sum(p, axis=0, keepdims=True)
        ot = _dot(vt_ref[...], p.astype(BF16)) / l
        o_ref[t * tq:(t + 1) * tq, :] = ot.T.astype(BF16)


def _mla_attn(q, k, v, kx, vx, *, nc, dseq, dbatch, past, tq):
    off = nc // dseq
    kern = functools.partial(_mla_attn_kernel, tq=tq)
    return pl.pallas_call(
        kern,
        out_shape=jax.ShapeDtypeStruct((dbatch * dseq, H_C * V_DIM), BF16),
        grid=(dbatch, H_C),
        in_specs=[
            pl.BlockSpec((dseq, QK_PAD), lambda b, h: (off + b, h)),
            pl.BlockSpec((dseq, QK_PAD), lambda b, h: (off + b, h)),
            pl.BlockSpec((dseq, V_DIM), lambda b, h: (off + b, h)),
            pl.BlockSpec((past, QK_PAD), lambda b, h: (b, h)),
            pl.BlockSpec((past, V_DIM), lambda b, h: (b, h)),
        ],
        out_specs=pl.BlockSpec((dseq, V_DIM), lambda b, h: (b, h)),
        scratch_shapes=[pltpu.VMEM((past + dseq, QK_PAD), BF16),
                        pltpu.VMEM((V_DIM, past + dseq), BF16)],
        compiler_params=_params("arbitrary", "arbitrary"),
        name="mla_attn",
    )(q, k, v, kx, vx)


def _rope_tables(dseq, tm):
    t = np.arange(dseq)
    quarter = ROPE // 4
    freqs = jnp.asarray(ROPE_BASE, F32) ** (-jnp.arange(quarter, dtype=F32) / quarter)
    ang_r = jnp.asarray(t // GRID_W, F32)[:, None] * freqs
    ang_c = jnp.asarray(t % GRID_W, F32)[:, None] * freqs
    zeros = jnp.zeros((dseq, LANES - ROPE), F32)
    cos = jnp.concatenate([jnp.cos(ang_r), jnp.cos(ang_r), jnp.cos(ang_c), jnp.cos(ang_c), zeros], axis=1)
    sin = jnp.concatenate([-jnp.sin(ang_r), jnp.sin(ang_r), -jnp.sin(ang_c), jnp.sin(ang_c), zeros], axis=1)
    ident = jnp.concatenate([jnp.ones((tm, ROPE), F32), jnp.zeros((tm, LANES - ROPE), F32)], axis=1)
    return (jnp.concatenate([ident, cos], axis=0),
            jnp.concatenate([jnp.zeros((tm, LANES), F32), sin], axis=0))


def _pick_tile(pref, *lengths):
    t = pref
    while any(l % t for l in lengths):
        t //= 2
    return t


def kernel(x_prompt, x_sample, cache_nat_k, cache_nat_v, cache_mla_ckv, cache_mla_krope, c, c_ctx,
           norm1_g, norm2_g, w_ada, b_ada, w_in_ab, g_qn_a, g_kn_a, rpb_a, conv_b_w, w_out_ab,
           w_down_c, g_cq, g_ckv, w_uq_c, w_ukv_c, g_qn_c, g_kn_c, w_o_c, w_ffn_in, w_ffn_out):
    batch, seq, d = x_prompt.shape
    dbatch, dseq, _ = x_sample.shape
    depth = w_ada.shape[0]
    n_even = w_in_ab.shape[0]
    past = cache_nat_k.shape[3]
    nc, ns = batch * seq, dbatch * dseq
    n = nc + ns
    dff = w_ffn_out.shape[1]
    assert nc % dseq == 0 and dseq % seq == 0 and dseq % GRID_W == 0

    tm = _pick_tile(512, nc, dseq)
    tm_c = _pick_tile(256, nc, dseq)
    tf = _pick_tile(512, dff)
    assert tm % seq == 0

    x = jnp.concatenate([x_prompt.reshape(nc, d), x_sample.reshape(ns, d)], axis=0)

    groups = 1 + dbatch
    gp = -(-groups // SUBLANES) * SUBLANES
    cond = jnp.concatenate([c_ctx[None, :], c, jnp.zeros((gp - groups, d), F32)], axis=0)
    mods_all = _adaln(cond, w_ada, b_ada).reshape(depth, gp, 6, d)

    rope_c, rope_s = _rope_tables(dseq, tm_c)
    ident_c = jnp.concatenate([jnp.ones((past, ROPE), F32), jnp.zeros((past, LANES - ROPE), F32)], axis=1)
    shape_kw = dict(nc=nc, dseq=dseq)

    w_in_ab_b = w_in_ab.astype(BF16)
    w_out_ab_b = w_out_ab.astype(BF16)
    w_o_c_b = w_o_c.astype(BF16)
    w_ffn_in_b = w_ffn_in.astype(BF16)
    w_ffn_out_b = w_ffn_out.astype(BF16)

    nat = None
    mla_ckv, mla_kr = [], []
    out = None
    for l in range(depth):
        i = l // 2
        mods = mods_all[l]
        if l % 2 == 0:
            gq = (g_qn_a[i] * (DH_A ** -0.5 * LOG2E))[None, :]
            qkv, gb, z, nk, nv = _inproj_ab(
                x, mods, norm1_g[l][None, :], w_in_ab_b, gq, g_kn_a[i][None, :], nat, i,
                seq=seq, batch=batch, n_even=n_even, tm=tm, **shape_kw)
            nat = (nk, nv)
            ya_c = _ctx_attn(qkv, qkv, qkv, (0, 1, 2), nc=nc, batch=batch, seq=seq,
                             heads=H_A, dq=DH_A, dv=DH_A)
            ya_s = _nat_attn(rpb_a[i], qkv, cache_nat_k, cache_nat_v, i, dbatch=dbatch, **shape_kw)
            x = _outproj(x, mods, ya_c, ya_s, w_out_ab_b, i, (gb, z, conv_b_w[i]),
                         seq=seq, tm=tm, **shape_kw)
        else:
            wd = jnp.pad(w_down_c[i], ((0, 0), (0, LANES - ROPE))).astype(BF16)
            wq = jnp.pad(w_uq_c[i].reshape(Q_LORA, H_C, QK_DIM),
                         ((0, 0), (0, 0), (0, QK_PAD - QK_DIM))).reshape(Q_LORA, H_C * QK_PAD).astype(BF16)
            wkv = w_ukv_c[i].astype(BF16)
            gq = jnp.pad(g_qn_c[i] * (QK_DIM ** -0.5 * LOG2E), (0, QK_PAD - QK_DIM))[None, :]
            gkn = g_kn_c[i][None, :NOPE]
            gkr = jnp.pad(g_kn_c[i][NOPE:], (0, LANES - ROPE))[None, :]
            q, k, v, ckv_n, kr = _inproj_c(
                x, mods, norm1_g[l][None, :], wd, g_cq[i][None, :], g_ckv[i][None, :],
                wq, gq, wkv, gkn, gkr, rope_c, rope_s, tm=tm_c, **shape_kw)
            mla_ckv.append(ckv_n.reshape(batch, seq, KV_LORA))
            mla_kr.append(kr.reshape(batch, seq, ROPE))
            kx, vx = _cache_kv(
                cache_mla_ckv[:, i].reshape(dbatch * past, KV_LORA),
                jnp.pad(cache_mla_krope[:, i].reshape(dbatch * past, ROPE), ((0, 0), (0, LANES - ROPE))),
                wkv, gkn, gkr, ident_c, jnp.zeros((past, LANES), F32), tm=past)
            ya_c = _ctx_attn(q, k, v, (0, 0, 0), nc=nc, batch=batch, seq=seq,
                             heads=H_C, dq=QK_PAD, dv=V_DIM)
            ya_s = _mla_attn(q, k, v, kx, vx, dbatch=dbatch, past=past,
                             tq=_pick_tile(256, dseq), **shape_kw)
            x = _outproj(x, mods, ya_c, ya_s, w_o_c_b, i, seq=seq, tm=tm, **shape_kw)
        last = l == depth - 1
        res = _ffn(x, mods, norm2_g[l][None, :], w_ffn_in_b, w_ffn_out_b, l,
                   split=last, tm=tm, tf=tf, **shape_kw)
        if last:
            out = res
        else:
            x = res

    return (out[0].reshape(batch, seq, d), out[1].reshape(dbatch, dseq, d), nat[0], nat[1],
            jnp.stack(mla_ckv, axis=1), jnp.stack(mla_kr, axis=1))
```

```python
import functools

import numpy as np
import jax
import jax.numpy as jnp
from jax import lax
from jax.experimental import pallas as pl
from jax.experimental.pallas import tpu as pltpu

F32 = jnp.float32
BF16 = jnp.bfloat16

EPS = 1e-6
GRID_W = 64
WIN_R = 8
WIN_C = 16
ROPE_BASE = 10000.0
H_A = 8
DH_A = 128
H_C = 16
Q_LORA = 512
KV_LORA = 256
NOPE = 128
ROPE = 64
V_DIM = 128
QK_DIM = NOPE + ROPE
QK_PAD = 256
LANES = 128
SUBLANES = 8
NEG = -1e30
LOG2E = 1.4426950408889634

VMEM_LIMIT = 56 * 1024 * 1024
ROW_BLOCK = 128


def _params(*sem):
    return pltpu.CompilerParams(dimension_semantics=sem, vmem_limit_bytes=VMEM_LIMIT)


def _group(i, nct, per):
    return jnp.where(i < nct, 0, 1 + (i - nct) // per)


def _modulated(x, g, shift, scale):
    ms = jnp.mean(x * x, axis=-1, keepdims=True)
    return (x * lax.rsqrt(ms + EPS) * g) * (1.0 + scale) + shift


def _dot(a, b):
    return jnp.dot(a, b, preferred_element_type=F32)


def _dot_nt(a, b):
    return lax.dot_general(a, b, (((1,), (1,)), ((), ())), preferred_element_type=F32)


def _ada_kernel(c_ref, w_ref, b_ref, o_ref):
    c = c_ref[...]
    s = (c / (1.0 + jnp.exp(-c))).astype(BF16)
    o_ref[...] = _dot(s, w_ref[...].astype(BF16)) + b_ref[...]


def _adaln(cond, w_ada, b_ada):
    depth, d, n6 = w_ada.shape
    gp = cond.shape[0]
    tn = 1024 if n6 % 1024 == 0 else n6
    return pl.pallas_call(
        _ada_kernel,
        out_shape=jax.ShapeDtypeStruct((depth, gp, n6), F32),
        grid=(depth, n6 // tn),
        in_specs=[
            pl.BlockSpec((gp, d), lambda l, n: (0, 0)),
            pl.BlockSpec((None, d, tn), lambda l, n: (l, 0, n)),
            pl.BlockSpec((None, 1, tn), lambda l, n: (l, 0, n)),
        ],
        out_specs=pl.BlockSpec((None, gp, tn), lambda l, n: (l, 0, n)),
        compiler_params=_params("arbitrary", "arbitrary"),
        name="adaln",
    )(cond, w_ada, b_ada.reshape(depth, 1, n6))


def _inproj_ab_kernel(x_ref, mod_ref, g1_ref, w_ref, gq_ref, gk_ref, *rest, nct, seq, first, rb):
    if first:
        qkv_ref, gb_ref, z_ref, natk_ref, natv_ref, h_ref, gc_ref = rest
    else:
        _, _, qkv_ref, gb_ref, z_ref, natk_ref, natv_ref, h_ref, gc_ref = rest
    i = pl.program_id(0)
    n = pl.program_id(1)
    tm = x_ref.shape[0]
    blocks = [slice(r * rb, (r + 1) * rb) for r in range(tm // rb)]

    def proj(rows):
        return _dot(h_ref[rows, :], w_ref[...])

    def nat_store(nat_ref, rows, hh, val):
        bb, off = rows.start // seq, rows.start % seq
        if first:
            nat_ref[bb, 0, hh, off:off + rb, :] = val
        else:
            nat_ref[bb, hh, off:off + rb, :] = val

    def head_norm(rows, r, gain_ref, nat_ref):
        for hh in range(H_A):
            t = r[:, hh * DH_A:(hh + 1) * DH_A]
            ms = jnp.mean(t * t, axis=-1, keepdims=True)
            val = t * lax.rsqrt(ms + EPS) * gain_ref[...]
            qkv_ref[rows, hh * DH_A:(hh + 1) * DH_A] = val.astype(BF16)
            if nat_ref is not None:
                nat_store(nat_ref, rows, hh, val)

    is_ctx = i < nct

    @pl.when(n == 0)
    def _():
        if first and natk_ref.shape[1] > 1:
            @pl.when(i < nct)
            def _():
                natk_ref[:, 1:] = jnp.zeros_like(natk_ref[:, 1:])
                natv_ref[:, 1:] = jnp.zeros_like(natv_ref[:, 1:])
        gs = g1_ref[...] * (1.0 + mod_ref[1:2, :])
        for rows in blocks:
            x = x_ref[rows, :]
            ms = jnp.mean(x * x, axis=-1, keepdims=True)
            h_ref[rows, :] = (x * lax.rsqrt(ms + EPS) * gs + mod_ref[0:1, :]).astype(BF16)
            head_norm(rows, proj(rows), gq_ref, None)

    def k_tile(nat_ref):
        for rows in blocks:
            head_norm(rows, proj(rows), gk_ref, nat_ref)

    pl.when((n == 1) & is_ctx)(lambda: k_tile(natk_ref))
    pl.when((n == 1) & jnp.logical_not(is_ctx))(lambda: k_tile(None))

    def v_tile(nat_ref):
        for rows in blocks:
            r = proj(rows)
            qkv_ref[rows, :] = r.astype(BF16)
            if nat_ref is not None:
                for hh in range(H_A):
                    nat_store(nat_ref, rows, hh, r[:, hh * DH_A:(hh + 1) * DH_A])

    pl.when((n == 2) & is_ctx)(lambda: v_tile(natv_ref))
    pl.when((n == 2) & jnp.logical_not(is_ctx))(lambda: v_tile(None))

    @pl.when(n == 3)
    def _():
        for rows in blocks:
            gb_ref[rows, :] = proj(rows)

    @pl.when(n == 4)
    def _():
        for rows in blocks:
            gc_ref[rows, :] = proj(rows)

    @pl.when(n == 5)
    def _():
        for rows in blocks:
            z_ref[rows, :] = gc_ref[rows, :] * proj(rows)


def _inproj_ab(x, mods, g1, w_in, gq, gk, nat_prev, slot, *, nc, seq, dseq, batch, n_even, tm):
    n, d = x.shape
    na = H_A * DH_A
    nct = nc // tm
    per = dseq // tm
    first = nat_prev is None
    kern = functools.partial(_inproj_ab_kernel, nct=nct, seq=seq, first=first,
                             rb=_pick_tile(ROW_BLOCK, tm, seq))
    nat_shape = jax.ShapeDtypeStruct((batch, n_even, H_A, seq, DH_A), F32)
    if first:
        nat_spec = pl.BlockSpec((tm // seq, n_even, H_A, seq, DH_A),
                                lambda i, j: (jnp.minimum(i, nct - 1), 0, 0, 0, 0))
    else:
        nat_spec = pl.BlockSpec((tm // seq, None, H_A, seq, DH_A),
                                lambda i, j: (jnp.minimum(i, nct - 1), slot, 0, 0, 0))
    in_specs = [
        pl.BlockSpec((tm, d), lambda i, j: (i, 0)),
        pl.BlockSpec((None, 6, d), lambda i, j: (_group(i, nct, per), 0, 0)),
        pl.BlockSpec((1, d), lambda i, j: (0, 0)),
        pl.BlockSpec((None, d, na), lambda i, j: (slot, 0, j)),
        pl.BlockSpec((1, DH_A), lambda i, j: (0, 0)),
        pl.BlockSpec((1, DH_A), lambda i, j: (0, 0)),
    ]
    args = [x, mods, g1, w_in, gq, gk]
    aliases = {}
    if not first:
        in_specs += [pl.BlockSpec(memory_space=pl.ANY)] * 2
        args += list(nat_prev)
        aliases = {6: 3, 7: 4}
    return pl.pallas_call(
        kern,
        out_shape=(jax.ShapeDtypeStruct((n, 3 * na), BF16),
                   jax.ShapeDtypeStruct((n, na), F32),
                   jax.ShapeDtypeStruct((n, na), F32),
                   nat_shape, nat_shape),
        grid=(n // tm, 6),
        in_specs=in_specs,
        out_specs=(
            pl.BlockSpec((tm, na), lambda i, j: (i, jnp.minimum(j, 2))),
            pl.BlockSpec((tm, na), lambda i, j: (i, 0)),
            pl.BlockSpec((tm, na), lambda i, j: (i, 0)),
            nat_spec, nat_spec,
        ),
        scratch_shapes=[pltpu.VMEM((tm, d), BF16), pltpu.VMEM((tm, na), F32)],
        input_output_aliases=aliases,
        compiler_params=_params("arbitrary", "arbitrary"),
        name="inproj_ab",
    )(*args)


def _ctx_attn_kernel(q_ref, k_ref, v_ref, o_ref, *, heads, dq, dv):
    def scores(hh):
        return _dot_nt(q_ref[:, hh * dq:(hh + 1) * dq], k_ref[:, hh * dq:(hh + 1) * dq])

    s_next = scores(0)
    for hh in range(heads):
        s = s_next
        if hh + 1 < heads:
            s_next = scores(hh + 1)
        m = jnp.max(s, axis=-1, keepdims=True)
        p = jnp.exp2(s - m)
        l = jnp.sum(p, axis=-1, keepdims=True)
        o = _dot(p.astype(BF16), v_ref[:, hh * dv:(hh + 1) * dv]) / l
        o_ref[:, hh * dv:(hh + 1) * dv] = o.astype(BF16)


def _ctx_attn(q_arr, k_arr, v_arr, cols, *, nc, batch, seq, heads, dq, dv):
    cq, ck, cv = cols
    kern = functools.partial(_ctx_attn_kernel, heads=heads, dq=dq, dv=dv)
    return pl.pallas_call(
        kern,
        out_shape=jax.ShapeDtypeStruct((nc, heads * dv), BF16),
        grid=(batch,),
        in_specs=[
            pl.BlockSpec((seq, heads * dq), lambda b: (b, cq)),
            pl.BlockSpec((seq, heads * dq), lambda b: (b, ck)),
            pl.BlockSpec((seq, heads * dv), lambda b: (b, cv)),
        ],
        out_specs=pl.BlockSpec((seq, heads * dv), lambda b: (b, 0)),
        compiler_params=_params("arbitrary"),
        name="ctx_attn",
    )(q_arr, k_arr, v_arr)


NAT_QR = 8
NAT_KR = 16


def _nat_window(a, rows):
    return int(np.clip(a * NAT_QR - WIN_R // 2, 0, rows - NAT_KR))


def _nat_attn_kernel(rpb_ref, q_ref, k_ref, v_ref, ck_ref, cv_ref, o_ref,
                     t2_ref, bias_ref, *, rows):
    hh = pl.program_id(0)
    b = pl.program_id(1)
    nblk = rows // NAT_QR
    n_dr = 2 * WIN_R - 1
    n_dc = 2 * WIN_C - 1

    @pl.when(b == 0)
    def _():
        j = lax.broadcasted_iota(jnp.int32, (GRID_W, LANES), 0)
        kc = lax.broadcasted_iota(jnp.int32, (GRID_W, LANES), 1) % GRID_W
        cs = jnp.clip(j - WIN_C // 2, 0, GRID_W - WIN_C)
        col_in = (kc >= cs) & (kc < cs + WIN_C)
        dc = kc - j + (WIN_C - 1)
        for d in range(n_dr):
            t = jnp.full((GRID_W, LANES), NEG, F32)
            for c in range(n_dc):
                t = jnp.where(col_in & (dc == c), rpb_ref[hh, d * n_dc + c] * LOG2E, t)
            t2_ref[d] = t
        lane = lax.broadcasted_iota(jnp.int32, (GRID_W, LANES), 1)
        negt = jnp.full((GRID_W, LANES), NEG, F32)
        for a in range(nblk):
            ws = _nat_window(a, rows)
            for qr in range(NAT_QR):
                r = a * NAT_QR + qr
                rs = int(np.clip(r - WIN_R // 2, 0, rows - WIN_R))
                for pp in range(NAT_KR // 2):
                    halves = []
                    for kk in (2 * pp, 2 * pp + 1):
                        kr = ws + kk
                        halves.append(t2_ref[kr - r + WIN_R - 1] if rs <= kr < rs + WIN_R else None)
                    if halves[0] is None and halves[1] is None:
                        blk = negt
                    else:
                        left = negt if halves[0] is None else halves[0]
                        right = negt if halves[1] is None else halves[1]
                        blk = jnp.where(lane < GRID_W, left, right)
                    bias_ref[a, qr * GRID_W:(qr + 1) * GRID_W, pp * LANES:(pp + 1) * LANES] = blk

    ck = ck_ref[...].astype(BF16)
    cv = cv_ref[...].astype(BF16)
    tq = NAT_QR * GRID_W
    tk = NAT_KR * GRID_W

    def scores(a):
        ws = _nat_window(a, rows)
        q = q_ref[a * tq:(a + 1) * tq, :]
        kw = k_ref[ws * GRID_W:ws * GRID_W + tk, :]
        return _dot_nt(q, kw) + bias_ref[a], _dot_nt(q, ck)

    s_next = scores(0)
    for a in range(nblk):
        s_loc, s_ctx = s_next
        if a + 1 < nblk:
            s_next = scores(a + 1)
        ws = _nat_window(a, rows)
        vw = v_ref[ws * GRID_W:ws * GRID_W + tk, :]
        m = jnp.maximum(jnp.max(s_loc, axis=-1, keepdims=True),
                        jnp.max(s_ctx, axis=-1, keepdims=True))
        p_loc = jnp.exp2(s_loc - m)
        p_ctx = jnp.exp2(s_ctx - m)
        l = jnp.sum(p_loc, axis=-1, keepdims=True) + jnp.sum(p_ctx, axis=-1, keepdims=True)
        o = (_dot(p_loc.astype(BF16), vw) + _dot(p_ctx.astype(BF16), cv)) / l
        o_ref[a * tq:(a + 1) * tq, :] = o.astype(BF16)


def _nat_attn(rpb, qkv, cache_k, cache_v, slot, *, nc, dseq, dbatch):
    rows = dseq // GRID_W
    assert rows % NAT_QR == 0 and rows >= NAT_KR
    nblk = rows // NAT_QR
    past = cache_k.shape[3]
    off = nc // dseq
    n_dr, n_dc = 2 * WIN_R - 1, 2 * WIN_C - 1
    kern = functools.partial(_nat_attn_kernel, rows=rows)
    cache_spec = pl.BlockSpec((None, None, None, past, DH_A), lambda h, b: (b, slot, h, 0, 0))
    return pl.pallas_call(
        kern,
        out_shape=jax.ShapeDtypeStruct((dbatch * dseq, H_A * DH_A), BF16),
        grid=(H_A, dbatch),
        in_specs=[
            pl.BlockSpec(memory_space=pltpu.SMEM),
            pl.BlockSpec((dseq, DH_A), lambda h, b: (off + b, h)),
            pl.BlockSpec((dseq, DH_A), lambda h, b: (off + b, H_A + h)),
            pl.BlockSpec((dseq, DH_A), lambda h, b: (off + b, 2 * H_A + h)),
            cache_spec, cache_spec,
        ],
        out_specs=pl.BlockSpec((dseq, DH_A), lambda h, b: (b, h)),
        scratch_shapes=[pltpu.VMEM((n_dr, GRID_W, LANES), F32),
                        pltpu.VMEM((nblk, NAT_QR * GRID_W, NAT_KR * GRID_W), F32)],
        compiler_params=_params("arbitrary", "arbitrary"),
        name="nat_attn",
    )(rpb.reshape(H_A, n_dr * n_dc), qkv, qkv, qkv, cache_k, cache_v)


def _outproj_kernel(*refs, conv, nct, seq, dseq, rb):
    if conv:
        (x_ref, mod_ref, g2_ref, yac_ref, yas_ref, gb_ref, z_ref, zp_ref, zn_ref, cw_ref, w_ref,
         o_ref, h2_ref, zs_ref) = refs
    else:
        x_ref, mod_ref, g2_ref, yac_ref, yas_ref, w_ref, o_ref, h2_ref = refs
    i = pl.program_id(0)
    tm = x_ref.shape[0]
    ka = yac_ref.shape[1]
    if conv:
        h8 = SUBLANES
        zs_ref[0:h8, :] = zp_ref[...]
        zs_ref[h8:h8 + tm, :] = z_ref[...]
        zs_ref[h8 + tm:2 * h8 + tm, :] = zn_ref[...]
        last = jnp.where(i < nct, seq - 1, dseq - 1)
    gate = mod_ref[2:3, :]
    gs = g2_ref[...] * (1.0 + mod_ref[4:5, :])
    shift = mod_ref[3:4, :]
    for r0 in range(0, tm, rb):
        rows = slice(r0, r0 + rb)
        ya = jnp.where(i < nct, yac_ref[rows, :], yas_ref[rows, :])
        y = _dot(ya, w_ref[0:ka, :])
        if conv:
            row = r0 + lax.broadcasted_iota(jnp.int32, (rb, 1), 0)
            pos = jnp.where(i < nct, row % seq, ((i - nct) * tm + row) % dseq)
            z_prev = jnp.where(pos == 0, 0.0, zs_ref[h8 - 1 + r0:h8 - 1 + r0 + rb, :])
            z_next = jnp.where(pos == last, 0.0, zs_ref[h8 + 1 + r0:h8 + 1 + r0 + rb, :])
            cv = cw_ref[0:1, :] * z_prev + cw_ref[1:2, :] * z_ref[rows, :] + cw_ref[2:3, :] * z_next
            yb = gb_ref[rows, :] * cv
            y = y + _dot(yb.astype(BF16), w_ref[ka:, :])
        xn = x_ref[rows, :] + gate * y
        o_ref[rows, :] = xn
        ms = jnp.mean(xn * xn, axis=-1, keepdims=True)
        h2_ref[rows, :] = (xn * lax.rsqrt(ms + EPS) * gs + shift).astype(BF16)


def _outproj(x, mods, g2, ya_c, ya_s, w_out, slot, conv_in=None, *, nc, seq, dseq, tm):
    n, d = x.shape
    ka = ya_c.shape[1]
    nct = nc // tm
    per = dseq // tm
    conv = conv_in is not None
    kern = functools.partial(_outproj_kernel, conv=conv, nct=nct, seq=seq, dseq=dseq,
                             rb=_pick_tile(ROW_BLOCK, tm))
    in_specs = [
        pl.BlockSpec((tm, d), lambda i: (i, 0)),
        pl.BlockSpec((None, 6, d), lambda i: (_group(i, nct, per), 0, 0)),
        pl.BlockSpec((1, d), lambda i: (0, 0)),
        pl.BlockSpec((tm, ka), lambda i: (jnp.minimum(i, nct - 1), 0)),
        pl.BlockSpec((tm, ka), lambda i: (jnp.maximum(i - nct, 0), 0)),
    ]
    args = [x, mods, g2, ya_c, ya_s]
    scratch = []
    if conv:
        gb, z, cw = conv_in
        nb = z.shape[1]
        t8 = tm // SUBLANES
        n8 = n // SUBLANES
        in_specs += [
            pl.BlockSpec((tm, nb), lambda i: (i, 0)),
            pl.BlockSpec((tm, nb), lambda i: (i, 0)),
            pl.BlockSpec((SUBLANES, nb), lambda i: (jnp.maximum(i * t8 - 1, 0), 0)),
            pl.BlockSpec((SUBLANES, nb), lambda i: (jnp.minimum((i + 1) * t8, n8 - 1), 0)),
            pl.BlockSpec((3, nb), lambda i: (0, 0)),
        ]
        args += [gb, z, z, z, cw]
        scratch = [pltpu.VMEM((tm + 2 * SUBLANES, nb), F32)]
    in_specs.append(pl.BlockSpec((None,) + w_out.shape[1:], lambda i: (slot, 0, 0),
                                 pipeline_mode=pl.Buffered(1)))
    args.append(w_out)
    return pl.pallas_call(
        kern,
        out_shape=(jax.ShapeDtypeStruct((n, d), F32), jax.ShapeDtypeStruct((n, d), BF16)),
        grid=(n // tm,),
        in_specs=in_specs,
        out_specs=(pl.BlockSpec((tm, d), lambda i: (i, 0)), pl.BlockSpec((tm, d), lambda i: (i, 0))),
        scratch_shapes=scratch,
        compiler_params=_params("arbitrary"),
        name="outproj_conv" if conv else "outproj",
    )(*args)


def _ffn_kernel(x_ref, h_ref, mod_ref, wg_ref, wu_ref, wo_ref, *rest, split, nct):
    if split:
        oc_ref, os_ref, acc_ref = rest
    else:
        (acc_ref,) = rest
    i = pl.program_id(0)
    f = pl.program_id(1)

    def step(start):
        h = h_ref[...]
        g = _dot(h, wg_ref[...])
        u = _dot(h, wu_ref[...])
        a = (g / (1.0 + jnp.exp(-g))) * u
        part = _dot(a.astype(BF16), wo_ref[...])
        acc_ref[...] = part if start else acc_ref[...] + part

    pl.when(f == 0)(lambda: step(True))
    pl.when(f != 0)(lambda: step(False))

    @pl.when(f == pl.num_programs(1) - 1)
    def _():
        res = x_ref[...] + mod_ref[5:6, :] * acc_ref[...]
        if split:
            @pl.when(i < nct)
            def _():
                oc_ref[...] = res

            @pl.when(i >= nct)
            def _():
                os_ref[...] = res
        else:
            acc_ref[...] = res


def _ffn(x, h2, mods, w_in, w_out, layer, *, split, nc, dseq, tm, tf):
    n, d = x.shape
    dff = w_out.shape[1]
    nf = dff // tf
    nct = nc // tm
    per = dseq // tm
    kern = functools.partial(_ffn_kernel, split=split, nct=nct)
    if split:
        out_shape = (jax.ShapeDtypeStruct((nc, d), F32), jax.ShapeDtypeStruct((n - nc, d), F32))
        out_specs = (pl.BlockSpec((tm, d), lambda i, f: (jnp.minimum(i, nct - 1), 0)),
                     pl.BlockSpec((tm, d), lambda i, f: (jnp.maximum(i - nct, 0), 0)))
        scratch = [pltpu.VMEM((tm, d), F32)]
    else:
        out_shape = jax.ShapeDtypeStruct((n, d), F32)
        out_specs = pl.BlockSpec((tm, d), lambda i, f: (i, 0))
        scratch = []
    return pl.pallas_call(
        kern,
        out_shape=out_shape,
        grid=(n // tm, nf),
        in_specs=[
            pl.BlockSpec((tm, d), lambda i, f: (i, 0)),
            pl.BlockSpec((tm, d), lambda i, f: (i, 0)),
            pl.BlockSpec((None, 6, d), lambda i, f: (_group(i, nct, per), 0, 0)),
            pl.BlockSpec((None, None, d, tf), lambda i, f: (layer, f, 0, 0)),
            pl.BlockSpec((None, None, d, tf), lambda i, f: (layer, nf + f, 0, 0)),
            pl.BlockSpec((None, tf, d), lambda i, f: (layer, f, 0)),
        ],
        out_specs=out_specs,
        scratch_shapes=scratch,
        compiler_params=_params("arbitrary", "arbitrary"),
        name="ffn",
    )(x, h2, mods, w_in, w_in, w_out)


def _rope(x, c, s):
    lane = lax.broadcasted_iota(jnp.int32, x.shape, 1)
    quarter = ROPE // 4
    swapped = jnp.where(lane % (2 * quarter) < quarter,
                        pltpu.roll(x, LANES - quarter, 1), pltpu.roll(x, quarter, 1))
    return x * c + swapped * s


def _mla_kv_expand(ckv_bf16, krp, wkv_ref, gkn_ref, gkr_ref, c, s, k_ref, v_ref, rows=slice(None)):
    kv = _dot(ckv_bf16, wkv_ref[...])
    krr = _rope(krp * gkr_ref[...], c, s)
    ss_kr = jnp.sum(krp * krp, axis=-1, keepdims=True)
    for hh in range(H_C):
        base = hh * (NOPE + V_DIM)
        kn = kv[:, base:base + NOPE]
        rstd = lax.rsqrt((jnp.sum(kn * kn, axis=-1, keepdims=True) + ss_kr) / QK_DIM + EPS)
        k_ref[rows, hh * QK_PAD:hh * QK_PAD + NOPE] = (kn * rstd * gkn_ref[...]).astype(BF16)
        k_ref[rows, hh * QK_PAD + NOPE:(hh + 1) * QK_PAD] = (krr * rstd).astype(BF16)
        v_ref[rows, hh * V_DIM:(hh + 1) * V_DIM] = kv[:, base + NOPE:base + NOPE + V_DIM].astype(BF16)


def _inproj_c_kernel(x_ref, mod_ref, g1_ref, wd_ref, gcq_ref, gckv_ref, wq_ref, gq_ref,
                     wkv_ref, gkn_ref, gkr_ref, c_ref, s_ref,
                     q_ref, k_ref, v_ref, ckv_ref, kr_ref, *, nct, rb):
    i = pl.program_id(0)
    tm = x_ref.shape[0]
    blocks = [slice(r * rb, (r + 1) * rb) for r in range(tm // rb)]
    gs = g1_ref[...] * (1.0 + mod_ref[1:2, :])

    def latents(rows):
        x = x_ref[rows, :]
        ms = jnp.mean(x * x, axis=-1, keepdims=True)
        h = (x * lax.rsqrt(ms + EPS) * gs + mod_ref[0:1, :]).astype(BF16)
        dn = _dot(h, wd_ref[...])
        cq = dn[:, :Q_LORA]
        ckv = dn[:, Q_LORA:Q_LORA + KV_LORA]
        krp = dn[:, Q_LORA + KV_LORA:]
        cqn = cq * lax.rsqrt(jnp.mean(cq * cq, axis=-1, keepdims=True) + EPS) * gcq_ref[...]
        ckvn = ckv * lax.rsqrt(jnp.mean(ckv * ckv, axis=-1, keepdims=True) + EPS) * gckv_ref[...]
        return cqn.astype(BF16), ckvn, krp

    def expand(rows, cqn, ckvn, krp, ctx):
        if ctx:
            ckv_ref[rows, :] = ckvn
            kr_ref[rows, :] = krp[:, :ROPE]
        c = c_ref[rows, :]
        s = s_ref[rows, :]
        q = _dot(cqn, wq_ref[...])
        for hh in range(H_C):
            qh = q[:, hh * QK_PAD:(hh + 1) * QK_PAD]
            rstd = lax.rsqrt(jnp.sum(qh * qh, axis=-1, keepdims=True) / QK_DIM + EPS)
            qn = qh * rstd * gq_ref[...]
            q_ref[rows, hh * QK_PAD:hh * QK_PAD + NOPE] = qn[:, :NOPE].astype(BF16)
            q_ref[rows, hh * QK_PAD + NOPE:(hh + 1) * QK_PAD] = _rope(qn[:, NOPE:], c, s).astype(BF16)
        _mla_kv_expand(ckvn.astype(BF16), krp, wkv_ref, gkn_ref, gkr_ref, c, s, k_ref, v_ref, rows)

    def tile(ctx):
        nxt = latents(blocks[0])
        for r, rows in enumerate(blocks):
            cur = nxt
            if r + 1 < len(blocks):
                nxt = latents(blocks[r + 1])
            expand(rows, *cur, ctx)

    pl.when(i < nct)(lambda: tile(True))
    pl.when(i >= nct)(lambda: tile(False))


def _inproj_c(x, mods, g1, wd, gcq, gckv, wq, gq, wkv, gkn, gkr, rope_c, rope_s,
              *, nc, dseq, tm):
    n, d = x.shape
    nct = nc // tm
    per = dseq // tm
    kern = functools.partial(_inproj_c_kernel, nct=nct, rb=_pick_tile(ROW_BLOCK, tm))
    const = lambda shape: pl.BlockSpec(shape, lambda i: (0,) * len(shape),
                                       pipeline_mode=pl.Buffered(1))
    rope_idx = lambda i: (jnp.where(i < nct, 0, 1 + (i - nct) % per), 0)
    ctx_idx = lambda i: (jnp.minimum(i, nct - 1), 0)
    return pl.pallas_call(
        kern,
        out_shape=(jax.ShapeDtypeStruct((n, H_C * QK_PAD), BF16),
                   jax.ShapeDtypeStruct((n, H_C * QK_PAD), BF16),
                   jax.ShapeDtypeStruct((n, H_C * V_DIM), BF16),
                   jax.ShapeDtypeStruct((nc, KV_LORA), F32),
                   jax.ShapeDtypeStruct((nc, ROPE), F32)),
        grid=(n // tm,),
        in_specs=[
            pl.BlockSpec((tm, d), lambda i: (i, 0)),
            pl.BlockSpec((None, 6, d), lambda i: (_group(i, nct, per), 0, 0)),
            const((1, d)), const(wd.shape), const((1, Q_LORA)), const((1, KV_LORA)),
            const(wq.shape), const((1, QK_PAD)), const(wkv.shape),
            const((1, NOPE)), const((1, LANES)),
            pl.BlockSpec((tm, LANES), rope_idx),
            pl.BlockSpec((tm, LANES), rope_idx),
        ],
        out_specs=(
            pl.BlockSpec((tm, H_C * QK_PAD), lambda i: (i, 0)),
            pl.BlockSpec((tm, H_C * QK_PAD), lambda i: (i, 0)),
            pl.BlockSpec((tm, H_C * V_DIM), lambda i: (i, 0)),
            pl.BlockSpec((tm, KV_LORA), ctx_idx),
            pl.BlockSpec((tm, ROPE), ctx_idx),
        ),
        compiler_params=_params("arbitrary"),
        name="inproj_c",
    )(x, mods, g1, wd, gcq, gckv, wq, gq, wkv, gkn, gkr, rope_c, rope_s)


def _cache_kv_kernel(ckv_ref, krp_ref, wkv_ref, gkn_ref, gkr_ref, c_ref, s_ref, k_ref, v_ref):
    _mla_kv_expand(ckv_ref[...].astype(BF16), krp_ref[...], wkv_ref, gkn_ref, gkr_ref,
                   c_ref[...], s_ref[...], k_ref, v_ref)


def _cache_kv(ckv, krp, wkv, gkn, gkr, rope_c, rope_s, *, tm):
    n = ckv.shape[0]
    const = lambda shape: pl.BlockSpec(shape, lambda i: (0,) * len(shape))
    return pl.pallas_call(
        _cache_kv_kernel,
        out_shape=(jax.ShapeDtypeStruct((n, H_C * QK_PAD), BF16),
                   jax.ShapeDtypeStruct((n, H_C * V_DIM), BF16)),
        grid=(n // tm,),
        in_specs=[
            pl.BlockSpec((tm, KV_LORA), lambda i: (i, 0)),
            pl.BlockSpec((tm, LANES), lambda i: (i, 0)),
            const(wkv.shape), const((1, NOPE)), const((1, LANES)),
            const((tm, LANES)), const((tm, LANES)),
        ],
        out_specs=(pl.BlockSpec((tm, H_C * QK_PAD), lambda i: (i, 0)),
                   pl.BlockSpec((tm, H_C * V_DIM), lambda i: (i, 0))),
        compiler_params=_params("arbitrary"),
        name="cache_kv",
    )(ckv, krp, wkv, gkn, gkr, rope_c, rope_s)


def _mla_attn_kernel(q_ref, k_ref, v_ref, kx_ref, vx_ref, o_ref, kall_ref, vt_ref, *, tq):
    past = kx_ref.shape[0]
    kall_ref[0:past, :] = kx_ref[...]
    kall_ref[past:, :] = k_ref[...]
    vt_ref[:, 0:past] = vx_ref[...].astype(F32).T.astype(BF16)
    vt_ref[:, past:] = v_ref[...].astype(F32).T.astype(BF16)

    def scores(t):
        q = q_ref[t * tq:(t + 1) * tq, :]
        return _dot_nt(kall_ref[...], q)

    nblk = q_ref.shape[0] // tq
    s_next = scores(0)
    for t in range(nblk):
        s = s_next
        if t + 1 < nblk:
            s_next = scores(t + 1)
        m = jnp.max(s, axis=0, keepdims=True)
        p = jnp.exp2(s - m)
        l = jnp.sum(p, axis=0, keepdims=True)
        ot = _dot(vt_ref[...], p.astype(BF16)) / l
        o_ref[t * tq:(t + 1) * tq, :] = ot.T.astype(BF16)


def _mla_attn(q, k, v, kx, vx, *, nc, dseq, dbatch, past, tq):
    off = nc // dseq
    kern = functools.partial(_mla_attn_kernel, tq=tq)
    return pl.pallas_call(
        kern,
        out_shape=jax.ShapeDtypeStruct((dbatch * dseq, H_C * V_DIM), BF16),
        grid=(dbatch, H_C),
        in_specs=[
            pl.BlockSpec((dseq, QK_PAD), lambda b, h: (off + b, h)),
            pl.BlockSpec((dseq, QK_PAD), lambda b, h: (off + b, h)),
            pl.BlockSpec((dseq, V_DIM), lambda b, h: (off + b, h)),
            pl.BlockSpec((past, QK_PAD), lambda b, h: (b, h)),
            pl.BlockSpec((past, V_DIM), lambda b, h: (b, h)),
        ],
        out_specs=pl.BlockSpec((dseq, V_DIM), lambda b, h: (b, h)),
        scratch_shapes=[pltpu.VMEM((past + dseq, QK_PAD), BF16),
                        pltpu.VMEM((V_DIM, past + dseq), BF16)],
        compiler_params=_params("arbitrary", "arbitrary"),
        name="mla_attn",
    )(q, k, v, kx, vx)


def _rope_tables(dseq, tm):
    t = np.arange(dseq)
    quarter = ROPE // 4
    freqs = jnp.asarray(ROPE_BASE, F32) ** (-jnp.arange(quarter, dtype=F32) / quarter)
    ang_r = jnp.asarray(t // GRID_W, F32)[:, None] * freqs
    ang_c = jnp.asarray(t % GRID_W, F32)[:, None] * freqs
    zeros = jnp.zeros((dseq, LANES - ROPE), F32)
    cos = jnp.concatenate([jnp.cos(ang_r), jnp.cos(ang_r), jnp.cos(ang_c), jnp.cos(ang_c), zeros], axis=1)
    sin = jnp.concatenate([-jnp.sin(ang_r), jnp.sin(ang_r), -jnp.sin(ang_c), jnp.sin(ang_c), zeros], axis=1)
    ident = jnp.concatenate([jnp.ones((tm, ROPE), F32), jnp.zeros((tm, LANES - ROPE), F32)], axis=1)
    return (jnp.concatenate([ident, cos], axis=0),
            jnp.concatenate([jnp.zeros((tm, LANES), F32), sin], axis=0))


def _pick_tile(pref, *lengths):
    t = pref
    while any(l % t for l in lengths):
        t //= 2
    return t


def kernel(x_prompt, x_sample, cache_nat_k, cache_nat_v, cache_mla_ckv, cache_mla_krope, c, c_ctx,
           norm1_g, norm2_g, w_ada, b_ada, w_in_ab, g_qn_a, g_kn_a, rpb_a, conv_b_w, w_out_ab,
           w_down_c, g_cq, g_ckv, w_uq_c, w_ukv_c, g_qn_c, g_kn_c, w_o_c, w_ffn_in, w_ffn_out):
    batch, seq, d = x_prompt.shape
    dbatch, dseq, _ = x_sample.shape
    depth = w_ada.shape[0]
    n_even = w_in_ab.shape[0]
    past = cache_nat_k.shape[3]
    nc, ns = batch * seq, dbatch * dseq
    n = nc + ns
    dff = w_ffn_out.shape[1]
    assert nc % dseq == 0 and dseq % seq == 0 and dseq % GRID_W == 0

    tm = _pick_tile(512, nc, dseq)
    tm_c = _pick_tile(512, nc, dseq)
    tf = _pick_tile(512, dff)
    assert tm % seq == 0

    x = jnp.concatenate([x_prompt.reshape(nc, d), x_sample.reshape(ns, d)], axis=0)

    groups = 1 + dbatch
    gp = -(-groups // SUBLANES) * SUBLANES
    cond = jnp.concatenate([c_ctx[None, :], c, jnp.zeros((gp - groups, d), F32)], axis=0)
    mods_all = _adaln(cond, w_ada, b_ada).reshape(depth, gp, 6, d)

    rope_c, rope_s = _rope_tables(dseq, tm_c)
    ident_c = jnp.concatenate([jnp.ones((past, ROPE), F32), jnp.zeros((past, LANES - ROPE), F32)], axis=1)
    shape_kw = dict(nc=nc, dseq=dseq)

    w_in_ab_b = w_in_ab.astype(BF16)
    w_out_ab_b = w_out_ab.astype(BF16)
    w_o_c_b = w_o_c.astype(BF16)
    w_ffn_in_b = w_ffn_in.astype(BF16).reshape(depth, d, 2 * (dff // tf), tf).transpose(0, 2, 1, 3)
    w_ffn_out_b = w_ffn_out.astype(BF16)

    nat = None
    mla_ckv, mla_kr = [], []
    out = None
    for l in range(depth):
        i = l // 2
        mods = mods_all[l]
        if l % 2 == 0:
            gq = (g_qn_a[i] * (DH_A ** -0.5 * LOG2E))[None, :]
            qkv, gb, z, nk, nv = _inproj_ab(
                x, mods, norm1_g[l][None, :], w_in_ab_b, gq, g_kn_a[i][None, :], nat, i,
                seq=seq, batch=batch, n_even=n_even, tm=tm, **shape_kw)
            nat = (nk, nv)
            ya_c = _ctx_attn(qkv, qkv, qkv, (0, 1, 2), nc=nc, batch=batch, seq=seq,
                             heads=H_A, dq=DH_A, dv=DH_A)
            ya_s = _nat_attn(rpb_a[i], qkv, cache_nat_k, cache_nat_v, i, dbatch=dbatch, **shape_kw)
            x, h2 = _outproj(x, mods, norm2_g[l][None, :], ya_c, ya_s, w_out_ab_b, i,
                             (gb, z, conv_b_w[i]), seq=seq, tm=tm, **shape_kw)
        else:
            wd = jnp.pad(w_down_c[i], ((0, 0), (0, LANES - ROPE))).astype(BF16)
            wq = jnp.pad(w_uq_c[i].reshape(Q_LORA, H_C, QK_DIM),
                         ((0, 0), (0, 0), (0, QK_PAD - QK_DIM))).reshape(Q_LORA, H_C * QK_PAD).astype(BF16)
            wkv = w_ukv_c[i].astype(BF16)
            gq = jnp.pad(g_qn_c[i] * (QK_DIM ** -0.5 * LOG2E), (0, QK_PAD - QK_DIM))[None, :]
            gkn = g_kn_c[i][None, :NOPE]
            gkr = jnp.pad(g_kn_c[i][NOPE:], (0, LANES - ROPE))[None, :]
            q, k, v, ckv_n, kr = _inproj_c(
                x, mods, norm1_g[l][None, :], wd, g_cq[i][None, :], g_ckv[i][None, :],
                wq, gq, wkv, gkn, gkr, rope_c, rope_s, tm=tm_c, **shape_kw)
            mla_ckv.append(ckv_n.reshape(batch, seq, KV_LORA))
            mla_kr.append(kr.reshape(batch, seq, ROPE))
            kx, vx = _cache_kv(
                cache_mla_ckv[:, i].reshape(dbatch * past, KV_LORA),
                jnp.pad(cache_mla_krope[:, i].reshape(dbatch * past, ROPE), ((0, 0), (0, LANES - ROPE))),
                wkv, gkn, gkr, ident_c, jnp.zeros((past, LANES), F32), tm=past)
            ya_c = _ctx_attn(q, k, v, (0, 0, 0), nc=nc, batch=batch, seq=seq,
                             heads=H_C, dq=QK_PAD, dv=V_DIM)
            ya_s = _mla_attn(q, k, v, kx, vx, dbatch=dbatch, past=past,
                             tq=_pick_tile(256, dseq), **shape_kw)
            x, h2 = _outproj(x, mods, norm2_g[l][None, :], ya_c, ya_s, w_o_c_b, i,
                             seq=seq, tm=tm, **shape_kw)
        last = l == depth - 1
        res = _ffn(x, h2, mods, w_ffn_in_b, w_ffn_out_b, l, split=last, tm=tm, tf=tf, **shape_kw)
        if last:
            out = res
        else:
            x = res

    return (out[0].reshape(batch, seq, d), out[1].reshape(dbatch, dseq, d), nat[0], nat[1],
            jnp.stack(mla_ckv, axis=1), jnp.stack(mla_kr, axis=1))
```

```python
import functools

import numpy as np
import jax
import jax.numpy as jnp
from jax import lax
from jax.experimental import pallas as pl
from jax.experimental.pallas import tpu as pltpu

F32 = jnp.float32
BF16 = jnp.bfloat16

EPS = 1e-6
GRID_W = 64
WIN_R = 8
WIN_C = 16
ROPE_BASE = 10000.0
H_A = 8
DH_A = 128
H_C = 16
Q_LORA = 512
KV_LORA = 256
NOPE = 128
ROPE = 64
V_DIM = 128
QK_DIM = NOPE + ROPE
QK_PAD = 256
LANES = 128
SUBLANES = 8
NEG = -1e30
LOG2E = 1.4426950408889634

VMEM_LIMIT = 56 * 1024 * 1024
ROW_BLOCK = 128


def _params(*sem):
    return pltpu.CompilerParams(dimension_semantics=sem, vmem_limit_bytes=VMEM_LIMIT)


def _group(i, nct, per):
    return jnp.where(i < nct, 0, 1 + (i - nct) // per)


def _modulated(x, g, shift, scale):
    ms = jnp.mean(x * x, axis=-1, keepdims=True)
    return (x * lax.rsqrt(ms + EPS) * g) * (1.0 + scale) + shift


def _dot(a, b):
    return jnp.dot(a, b, preferred_element_type=F32)


def _dot_nt(a, b):
    return lax.dot_general(a, b, (((1,), (1,)), ((), ())), preferred_element_type=F32)


def _ada_kernel(c_ref, w_ref, b_ref, o_ref):
    c = c_ref[...]
    s = (c / (1.0 + jnp.exp(-c))).astype(BF16)
    o_ref[...] = _dot(s, w_ref[...].astype(BF16)) + b_ref[...]


def _adaln(cond, w_ada, b_ada):
    depth, d, n6 = w_ada.shape
    gp = cond.shape[0]
    tn = 1024 if n6 % 1024 == 0 else n6
    return pl.pallas_call(
        _ada_kernel,
        out_shape=jax.ShapeDtypeStruct((depth, gp, n6), F32),
        grid=(depth, n6 // tn),
        in_specs=[
            pl.BlockSpec((gp, d), lambda l, n: (0, 0)),
            pl.BlockSpec((None, d, tn), lambda l, n: (l, 0, n)),
            pl.BlockSpec((None, 1, tn), lambda l, n: (l, 0, n)),
        ],
        out_specs=pl.BlockSpec((None, gp, tn), lambda l, n: (l, 0, n)),
        compiler_params=_params("arbitrary", "arbitrary"),
        name="adaln",
    )(cond, w_ada, b_ada.reshape(depth, 1, n6))


def _inproj_ab_kernel(x_ref, *rest, nct, seq, first, rb, pending):
    dx_ref = None
    if pending:
        dx_ref, *rest = rest
    mod_ref, g1_ref, w_ref, gq_ref, gk_ref, *rest = rest
    if first:
        qkv_ref, gb_ref, z_ref, natk_ref, natv_ref, h_ref, gc_ref = rest
    else:
        _, _, qkv_ref, gb_ref, z_ref, natk_ref, natv_ref, h_ref, gc_ref = rest
    i = pl.program_id(0)
    n = pl.program_id(1)
    tm = x_ref.shape[0]
    blocks = [slice(r * rb, (r + 1) * rb) for r in range(tm // rb)]

    def proj(rows):
        return _dot(h_ref[rows, :], w_ref[...])

    def nat_store(nat_ref, rows, hh, val):
        bb, off = rows.start // seq, rows.start % seq
        if first:
            nat_ref[bb, 0, hh, off:off + rb, :] = val
        else:
            nat_ref[bb, hh, off:off + rb, :] = val

    def head_norm(rows, r, gain_ref, nat_ref):
        for hh in range(H_A):
            t = r[:, hh * DH_A:(hh + 1) * DH_A]
            ms = jnp.mean(t * t, axis=-1, keepdims=True)
            val = t * lax.rsqrt(ms + EPS) * gain_ref[...]
            qkv_ref[rows, hh * DH_A:(hh + 1) * DH_A] = val.astype(BF16)
            if nat_ref is not None:
                nat_store(nat_ref, rows, hh, val)

    is_ctx = i < nct

    @pl.when(n == 0)
    def _():
        if first and natk_ref.shape[1] > 1:
            @pl.when(i < nct)
            def _():
                natk_ref[:, 1:] = jnp.zeros_like(natk_ref[:, 1:])
                natv_ref[:, 1:] = jnp.zeros_like(natv_ref[:, 1:])
        gs = g1_ref[...] * (1.0 + mod_ref[1:2, :])
        for rows in blocks:
            x = x_ref[rows, :]
            if pending:
                x = x + dx_ref[rows, :]
            ms = jnp.mean(x * x, axis=-1, keepdims=True)
            h_ref[rows, :] = (x * lax.rsqrt(ms + EPS) * gs + mod_ref[0:1, :]).astype(BF16)
            head_norm(rows, proj(rows), gq_ref, None)

    def k_tile(nat_ref):
        for rows in blocks:
            head_norm(rows, proj(rows), gk_ref, nat_ref)

    pl.when((n == 1) & is_ctx)(lambda: k_tile(natk_ref))
    pl.when((n == 1) & jnp.logical_not(is_ctx))(lambda: k_tile(None))

    def v_tile(nat_ref):
        for rows in blocks:
            r = proj(rows)
            qkv_ref[rows, :] = r.astype(BF16)
            if nat_ref is not None:
                for hh in range(H_A):
                    nat_store(nat_ref, rows, hh, r[:, hh * DH_A:(hh + 1) * DH_A])

    pl.when((n == 2) & is_ctx)(lambda: v_tile(natv_ref))
    pl.when((n == 2) & jnp.logical_not(is_ctx))(lambda: v_tile(None))

    @pl.when(n == 3)
    def _():
        for rows in blocks:
            gb_ref[rows, :] = proj(rows)

    @pl.when(n == 4)
    def _():
        for rows in blocks:
            gc_ref[rows, :] = proj(rows)

    @pl.when(n == 5)
    def _():
        for rows in blocks:
            z_ref[rows, :] = gc_ref[rows, :] * proj(rows)


def _inproj_ab(x, dx, mods, g1, w_in, gq, gk, nat_prev, slot, *, nc, seq, dseq, batch, n_even, tm):
    n, d = x.shape
    na = H_A * DH_A
    nct = nc // tm
    per = dseq // tm
    first = nat_prev is None
    pending = dx is not None
    kern = functools.partial(_inproj_ab_kernel, nct=nct, seq=seq, first=first,
                             rb=_pick_tile(ROW_BLOCK, tm, seq), pending=pending)
    nat_shape = jax.ShapeDtypeStruct((batch, n_even, H_A, seq, DH_A), F32)
    if first:
        nat_spec = pl.BlockSpec((tm // seq, n_even, H_A, seq, DH_A),
                                lambda i, j: (jnp.minimum(i, nct - 1), 0, 0, 0, 0))
    else:
        nat_spec = pl.BlockSpec((tm // seq, None, H_A, seq, DH_A),
                                lambda i, j: (jnp.minimum(i, nct - 1), slot, 0, 0, 0))
    in_specs = [pl.BlockSpec((tm, d), lambda i, j: (i, 0))] * (2 if pending else 1) + [
        pl.BlockSpec((None, 6, d), lambda i, j: (_group(i, nct, per), 0, 0)),
        pl.BlockSpec((1, d), lambda i, j: (0, 0)),
        pl.BlockSpec((None, d, na), lambda i, j: (slot, 0, j)),
        pl.BlockSpec((1, DH_A), lambda i, j: (0, 0)),
        pl.BlockSpec((1, DH_A), lambda i, j: (0, 0)),
    ]
    args = [x] + ([dx] if pending else []) + [mods, g1, w_in, gq, gk]
    aliases = {}
    if not first:
        in_specs += [pl.BlockSpec(memory_space=pl.ANY)] * 2
        aliases = {len(args): 3, len(args) + 1: 4}
        args += list(nat_prev)
    return pl.pallas_call(
        kern,
        out_shape=(jax.ShapeDtypeStruct((n, 3 * na), BF16),
                   jax.ShapeDtypeStruct((n, na), F32),
                   jax.ShapeDtypeStruct((n, na), F32),
                   nat_shape, nat_shape),
        grid=(n // tm, 6),
        in_specs=in_specs,
        out_specs=(
            pl.BlockSpec((tm, na), lambda i, j: (i, jnp.minimum(j, 2))),
            pl.BlockSpec((tm, na), lambda i, j: (i, 0)),
            pl.BlockSpec((tm, na), lambda i, j: (i, 0)),
            nat_spec, nat_spec,
        ),
        scratch_shapes=[pltpu.VMEM((tm, d), BF16), pltpu.VMEM((tm, na), F32)],
        input_output_aliases=aliases,
        compiler_params=_params("arbitrary", "arbitrary"),
        name="inproj_ab",
    )(*args)


def _ctx_attn_kernel(q_ref, k_ref, v_ref, o_ref, *, heads, dq, dv):
    def scores(hh):
        return _dot_nt(q_ref[:, hh * dq:(hh + 1) * dq], k_ref[:, hh * dq:(hh + 1) * dq])

    s_next = scores(0)
    for hh in range(heads):
        s = s_next
        if hh + 1 < heads:
            s_next = scores(hh + 1)
        m = jnp.max(s, axis=-1, keepdims=True)
        p = jnp.exp2(s - m)
        l = jnp.sum(p, axis=-1, keepdims=True)
        o = _dot(p.astype(BF16), v_ref[:, hh * dv:(hh + 1) * dv]) / l
        o_ref[:, hh * dv:(hh + 1) * dv] = o.astype(BF16)


def _ctx_attn(q_arr, k_arr, v_arr, cols, *, nc, batch, seq, heads, dq, dv):
    cq, ck, cv = cols
    kern = functools.partial(_ctx_attn_kernel, heads=heads, dq=dq, dv=dv)
    return pl.pallas_call(
        kern,
        out_shape=jax.ShapeDtypeStruct((nc, heads * dv), BF16),
        grid=(batch,),
        in_specs=[
            pl.BlockSpec((seq, heads * dq), lambda b: (b, cq)),
            pl.BlockSpec((seq, heads * dq), lambda b: (b, ck)),
            pl.BlockSpec((seq, heads * dv), lambda b: (b, cv)),
        ],
        out_specs=pl.BlockSpec((seq, heads * dv), lambda b: (b, 0)),
        compiler_params=_params("arbitrary"),
        name="ctx_attn",
    )(q_arr, k_arr, v_arr)


NAT_QR = 8
NAT_KR = 16


def _nat_window(a, rows):
    return int(np.clip(a * NAT_QR - WIN_R // 2, 0, rows - NAT_KR))


def _nat_attn_kernel(rpb_ref, q_ref, k_ref, v_ref, ck_ref, cv_ref, o_ref,
                     t2_ref, bias_ref, *, rows):
    hh = pl.program_id(0)
    b = pl.program_id(1)
    nblk = rows // NAT_QR
    n_dr = 2 * WIN_R - 1
    n_dc = 2 * WIN_C - 1

    @pl.when(b == 0)
    def _():
        j = lax.broadcasted_iota(jnp.int32, (GRID_W, LANES), 0)
        kc = lax.broadcasted_iota(jnp.int32, (GRID_W, LANES), 1) % GRID_W
        cs = jnp.clip(j - WIN_C // 2, 0, GRID_W - WIN_C)
        col_in = (kc >= cs) & (kc < cs + WIN_C)
        dc = kc - j + (WIN_C - 1)
        for d in range(n_dr):
            t = jnp.full((GRID_W, LANES), NEG, F32)
            for c in range(n_dc):
                t = jnp.where(col_in & (dc == c), rpb_ref[hh, d * n_dc + c] * LOG2E, t)
            t2_ref[d] = t
        lane = lax.broadcasted_iota(jnp.int32, (GRID_W, LANES), 1)
        negt = jnp.full((GRID_W, LANES), NEG, F32)
        for a in range(nblk):
            ws = _nat_window(a, rows)
            for qr in range(NAT_QR):
                r = a * NAT_QR + qr
                rs = int(np.clip(r - WIN_R // 2, 0, rows - WIN_R))
                for pp in range(NAT_KR // 2):
                    halves = []
                    for kk in (2 * pp, 2 * pp + 1):
                        kr = ws + kk
                        halves.append(t2_ref[kr - r + WIN_R - 1] if rs <= kr < rs + WIN_R else None)
                    if halves[0] is None and halves[1] is None:
                        blk = negt
                    else:
                        left = negt if halves[0] is None else halves[0]
                        right = negt if halves[1] is None else halves[1]
                        blk = jnp.where(lane < GRID_W, left, right)
                    bias_ref[a, qr * GRID_W:(qr + 1) * GRID_W, pp * LANES:(pp + 1) * LANES] = blk

    ck = ck_ref[...].astype(BF16)
    cv = cv_ref[...].astype(BF16)
    tq = NAT_QR * GRID_W
    tk = NAT_KR * GRID_W

    def scores(a):
        ws = _nat_window(a, rows)
        q = q_ref[a * tq:(a + 1) * tq, :]
        kw = k_ref[ws * GRID_W:ws * GRID_W + tk, :]
        return _dot_nt(q, kw) + bias_ref[a], _dot_nt(q, ck)

    s_next = scores(0)
    for a in range(nblk):
        s_loc, s_ctx = s_next
        if a + 1 < nblk:
            s_next = scores(a + 1)
        ws = _nat_window(a, rows)
        vw = v_ref[ws * GRID_W:ws * GRID_W + tk, :]
        m = jnp.maximum(jnp.max(s_loc, axis=-1, keepdims=True),
                        jnp.max(s_ctx, axis=-1, keepdims=True))
        p_loc = jnp.exp2(s_loc - m)
        p_ctx = jnp.exp2(s_ctx - m)
        l = jnp.sum(p_loc, axis=-1, keepdims=True) + jnp.sum(p_ctx, axis=-1, keepdims=True)
        o = (_dot(p_loc.astype(BF16), vw) + _dot(p_ctx.astype(BF16), cv)) / l
        o_ref[a * tq:(a + 1) * tq, :] = o.astype(BF16)


def _nat_attn(rpb, qkv, cache_k, cache_v, slot, *, nc, dseq, dbatch):
    rows = dseq // GRID_W
    assert rows % NAT_QR == 0 and rows >= NAT_KR
    nblk = rows // NAT_QR
    past = cache_k.shape[3]
    off = nc // dseq
    n_dr, n_dc = 2 * WIN_R - 1, 2 * WIN_C - 1
    kern = functools.partial(_nat_attn_kernel, rows=rows)
    cache_spec = pl.BlockSpec((None, None, None, past, DH_A), lambda h, b: (b, slot, h, 0, 0))
    return pl.pallas_call(
        kern,
        out_shape=jax.ShapeDtypeStruct((dbatch * dseq, H_A * DH_A), BF16),
        grid=(H_A, dbatch),
        in_specs=[
            pl.BlockSpec(memory_space=pltpu.SMEM),
            pl.BlockSpec((dseq, DH_A), lambda h, b: (off + b, h)),
            pl.BlockSpec((dseq, DH_A), lambda h, b: (off + b, H_A + h)),
            pl.BlockSpec((dseq, DH_A), lambda h, b: (off + b, 2 * H_A + h)),
            cache_spec, cache_spec,
        ],
        out_specs=pl.BlockSpec((dseq, DH_A), lambda h, b: (b, h)),
        scratch_shapes=[pltpu.VMEM((n_dr, GRID_W, LANES), F32),
                        pltpu.VMEM((nblk, NAT_QR * GRID_W, NAT_KR * GRID_W), F32)],
        compiler_params=_params("arbitrary", "arbitrary"),
        name="nat_attn",
    )(rpb.reshape(H_A, n_dr * n_dc), qkv, qkv, qkv, cache_k, cache_v)


def _outproj_kernel(x_ref, *refs, conv, nct, seq, dseq, rb, pending):
    dx_ref = None
    if pending:
        dx_ref, *refs = refs
    if conv:
        (mod_ref, g2_ref, yac_ref, yas_ref, gb_ref, z_ref, zp_ref, zn_ref, cw_ref, w_ref,
         o_ref, h2_ref, zs_ref) = refs
    else:
        mod_ref, g2_ref, yac_ref, yas_ref, w_ref, o_ref, h2_ref = refs
    i = pl.program_id(0)
    tm = x_ref.shape[0]
    ka = yac_ref.shape[1]
    if conv:
        h8 = SUBLANES
        zs_ref[0:h8, :] = zp_ref[...]
        zs_ref[h8:h8 + tm, :] = z_ref[...]
        zs_ref[h8 + tm:2 * h8 + tm, :] = zn_ref[...]
        last = jnp.where(i < nct, seq - 1, dseq - 1)
    gate = mod_ref[2:3, :]
    gs = g2_ref[...] * (1.0 + mod_ref[4:5, :])
    shift = mod_ref[3:4, :]
    for r0 in range(0, tm, rb):
        rows = slice(r0, r0 + rb)
        ya = jnp.where(i < nct, yac_ref[rows, :], yas_ref[rows, :])
        y = _dot(ya, w_ref[0:ka, :])
        if conv:
            row = r0 + lax.broadcasted_iota(jnp.int32, (rb, 1), 0)
            pos = jnp.where(i < nct, row % seq, ((i - nct) * tm + row) % dseq)
            z_prev = jnp.where(pos == 0, 0.0, zs_ref[h8 - 1 + r0:h8 - 1 + r0 + rb, :])
            z_next = jnp.where(pos == last, 0.0, zs_ref[h8 + 1 + r0:h8 + 1 + r0 + rb, :])
            cv = cw_ref[0:1, :] * z_prev + cw_ref[1:2, :] * z_ref[rows, :] + cw_ref[2:3, :] * z_next
            yb = gb_ref[rows, :] * cv
            y = y + _dot(yb.astype(BF16), w_ref[ka:, :])
        xn = x_ref[rows, :] + gate * y
        if pending:
            xn = xn + dx_ref[rows, :]
        o_ref[rows, :] = xn
        ms = jnp.mean(xn * xn, axis=-1, keepdims=True)
        h2_ref[rows, :] = (xn * lax.rsqrt(ms + EPS) * gs + shift).astype(BF16)


def _outproj(x, dx, mods, g2, ya_c, ya_s, w_out, slot, conv_in=None, *, nc, seq, dseq, tm):
    n, d = x.shape
    ka = ya_c.shape[1]
    nct = nc // tm
    per = dseq // tm
    conv = conv_in is not None
    pending = dx is not None
    kern = functools.partial(_outproj_kernel, conv=conv, nct=nct, seq=seq, dseq=dseq,
                             rb=_pick_tile(ROW_BLOCK, tm), pending=pending)
    in_specs = [pl.BlockSpec((tm, d), lambda i: (i, 0))] * (2 if pending else 1) + [
        pl.BlockSpec((None, 6, d), lambda i: (_group(i, nct, per), 0, 0)),
        pl.BlockSpec((1, d), lambda i: (0, 0)),
        pl.BlockSpec((tm, ka), lambda i: (jnp.minimum(i, nct - 1), 0)),
        pl.BlockSpec((tm, ka), lambda i: (jnp.maximum(i - nct, 0), 0)),
    ]
    args = [x] + ([dx] if pending else []) + [mods, g2, ya_c, ya_s]
    scratch = []
    if conv:
        gb, z, cw = conv_in
        nb = z.shape[1]
        t8 = tm // SUBLANES
        n8 = n // SUBLANES
        in_specs += [
            pl.BlockSpec((tm, nb), lambda i: (i, 0)),
            pl.BlockSpec((tm, nb), lambda i: (i, 0)),
            pl.BlockSpec((SUBLANES, nb), lambda i: (jnp.maximum(i * t8 - 1, 0), 0)),
            pl.BlockSpec((SUBLANES, nb), lambda i: (jnp.minimum((i + 1) * t8, n8 - 1), 0)),
            pl.BlockSpec((3, nb), lambda i: (0, 0)),
        ]
        args += [gb, z, z, z, cw]
        scratch = [pltpu.VMEM((tm + 2 * SUBLANES, nb), F32)]
    in_specs.append(pl.BlockSpec((None,) + w_out.shape[1:], lambda i: (slot, 0, 0),
                                 pipeline_mode=pl.Buffered(1)))
    args.append(w_out)
    return pl.pallas_call(
        kern,
        out_shape=(jax.ShapeDtypeStruct((n, d), F32), jax.ShapeDtypeStruct((n, d), BF16)),
        grid=(n // tm,),
        in_specs=in_specs,
        out_specs=(pl.BlockSpec((tm, d), lambda i: (i, 0)), pl.BlockSpec((tm, d), lambda i: (i, 0))),
        scratch_shapes=scratch,
        compiler_params=_params("arbitrary"),
        name="outproj_conv" if conv else "outproj",
    )(*args)


def _ffn_kernel(*refs, last, nct):
    if last:
        x_ref, h_ref, mod_ref, wg_ref, wu_ref, wo_ref, oc_ref, os_ref, acc_ref = refs
    else:
        h_ref, mod_ref, wg_ref, wu_ref, wo_ref, acc_ref = refs
    i = pl.program_id(0)
    f = pl.program_id(1)

    def step(start):
        h = h_ref[...]
        g = _dot(h, wg_ref[...])
        u = _dot(h, wu_ref[...])
        a = (g / (1.0 + jnp.exp(-g))) * u
        part = _dot(a.astype(BF16), wo_ref[...])
        acc_ref[...] = part if start else acc_ref[...] + part

    pl.when(f == 0)(lambda: step(True))
    pl.when(f != 0)(lambda: step(False))

    @pl.when(f == pl.num_programs(1) - 1)
    def _():
        delta = mod_ref[5:6, :] * acc_ref[...]
        if last:
            @pl.when(i < nct)
            def _():
                oc_ref[...] = x_ref[...] + delta

            @pl.when(i >= nct)
            def _():
                os_ref[...] = x_ref[...] + delta
        else:
            acc_ref[...] = delta


def _ffn(x, h2, mods, w_in, w_out, layer, *, last, nc, dseq, tm, tf):
    n, d = h2.shape
    dff = w_out.shape[1]
    nf = dff // tf
    nct = nc // tm
    per = dseq // tm
    kern = functools.partial(_ffn_kernel, last=last, nct=nct)
    row_spec = pl.BlockSpec((tm, d), lambda i, f: (i, 0))
    in_specs = [
        row_spec,
        pl.BlockSpec((None, 6, d), lambda i, f: (_group(i, nct, per), 0, 0)),
        pl.BlockSpec((None, d, tf), lambda i, f: (layer, 0, f)),
        pl.BlockSpec((None, d, tf), lambda i, f: (layer, 0, nf + f)),
        pl.BlockSpec((None, tf, d), lambda i, f: (layer, f, 0)),
    ]
    args = [h2, mods, w_in, w_in, w_out]
    if last:
        in_specs.insert(0, row_spec)
        args.insert(0, x)
        out_shape = (jax.ShapeDtypeStruct((nc, d), F32), jax.ShapeDtypeStruct((n - nc, d), F32))
        out_specs = (pl.BlockSpec((tm, d), lambda i, f: (jnp.minimum(i, nct - 1), 0)),
                     pl.BlockSpec((tm, d), lambda i, f: (jnp.maximum(i - nct, 0), 0)))
        scratch = [pltpu.VMEM((tm, d), F32)]
    else:
        out_shape = jax.ShapeDtypeStruct((n, d), F32)
        out_specs = row_spec
        scratch = []
    return pl.pallas_call(
        kern,
        out_shape=out_shape,
        grid=(n // tm, nf),
        in_specs=in_specs,
        out_specs=out_specs,
        scratch_shapes=scratch,
        compiler_params=_params("arbitrary", "arbitrary"),
        name="ffn",
    )(*args)


def _rope(x, c, s):
    lane = lax.broadcasted_iota(jnp.int32, x.shape, 1)
    quarter = ROPE // 4
    swapped = jnp.where(lane % (2 * quarter) < quarter,
                        pltpu.roll(x, LANES - quarter, 1), pltpu.roll(x, quarter, 1))
    return x * c + swapped * s


def _mla_kv_expand(ckv_bf16, krp, wkv_ref, gkn_ref, gkr_ref, c, s, k_ref, v_ref, rows=slice(None)):
    kv = _dot(ckv_bf16, wkv_ref[...])
    krr = _rope(krp * gkr_ref[...], c, s)
    ss_kr = jnp.sum(krp * krp, axis=-1, keepdims=True)
    for hh in range(H_C):
        base = hh * (NOPE + V_DIM)
        kn = kv[:, base:base + NOPE]
        rstd = lax.rsqrt((jnp.sum(kn * kn, axis=-1, keepdims=True) + ss_kr) / QK_DIM + EPS)
        k_ref[rows, hh * QK_PAD:hh * QK_PAD + NOPE] = (kn * rstd * gkn_ref[...]).astype(BF16)
        k_ref[rows, hh * QK_PAD + NOPE:(hh + 1) * QK_PAD] = (krr * rstd).astype(BF16)
        v_ref[rows, hh * V_DIM:(hh + 1) * V_DIM] = kv[:, base + NOPE:base + NOPE + V_DIM].astype(BF16)


def _inproj_c_kernel(x_ref, *rest, nct, rb, pending):
    dx_ref = None
    if pending:
        dx_ref, *rest = rest
    (mod_ref, g1_ref, wd_ref, gcq_ref, gckv_ref, wq_ref, gq_ref, wkv_ref, gkn_ref, gkr_ref,
     c_ref, s_ref, q_ref, k_ref, v_ref, ckv_ref, kr_ref) = rest
    i = pl.program_id(0)
    tm = x_ref.shape[0]
    blocks = [slice(r * rb, (r + 1) * rb) for r in range(tm // rb)]
    gs = g1_ref[...] * (1.0 + mod_ref[1:2, :])

    def latents(rows):
        x = x_ref[rows, :]
        if pending:
            x = x + dx_ref[rows, :]
        ms = jnp.mean(x * x, axis=-1, keepdims=True)
        h = (x * lax.rsqrt(ms + EPS) * gs + mod_ref[0:1, :]).astype(BF16)
        dn = _dot(h, wd_ref[...])
        cq = dn[:, :Q_LORA]
        ckv = dn[:, Q_LORA:Q_LORA + KV_LORA]
        krp = dn[:, Q_LORA + KV_LORA:]
        cqn = cq * lax.rsqrt(jnp.mean(cq * cq, axis=-1, keepdims=True) + EPS) * gcq_ref[...]
        ckvn = ckv * lax.rsqrt(jnp.mean(ckv * ckv, axis=-1, keepdims=True) + EPS) * gckv_ref[...]
        return cqn.astype(BF16), ckvn, krp

    def expand(rows, cqn, ckvn, krp, ctx):
        if ctx:
            ckv_ref[rows, :] = ckvn
            kr_ref[rows, :] = krp[:, :ROPE]
        c = c_ref[rows, :]
        s = s_ref[rows, :]
        q = _dot(cqn, wq_ref[...])
        for hh in range(H_C):
            qh = q[:, hh * QK_PAD:(hh + 1) * QK_PAD]
            rstd = lax.rsqrt(jnp.sum(qh * qh, axis=-1, keepdims=True) / QK_DIM + EPS)
            qn = qh * rstd * gq_ref[...]
            q_ref[rows, hh * QK_PAD:hh * QK_PAD + NOPE] = qn[:, :NOPE].astype(BF16)
            q_ref[rows, hh * QK_PAD + NOPE:(hh + 1) * QK_PAD] = _rope(qn[:, NOPE:], c, s).astype(BF16)
        _mla_kv_expand(ckvn.astype(BF16), krp, wkv_ref, gkn_ref, gkr_ref, c, s, k_ref, v_ref, rows)

    def tile(ctx):
        nxt = latents(blocks[0])
        for r, rows in enumerate(blocks):
            cur = nxt
            if r + 1 < len(blocks):
                nxt = latents(blocks[r + 1])
            expand(rows, *cur, ctx)

    pl.when(i < nct)(lambda: tile(True))
    pl.when(i >= nct)(lambda: tile(False))


def _inproj_c(x, dx, mods, g1, wd, gcq, gckv, wq, gq, wkv, gkn, gkr, rope_c, rope_s,
              *, nc, dseq, tm):
    n, d = x.shape
    nct = nc // tm
    per = dseq // tm
    pending = dx is not None
    kern = functools.partial(_inproj_c_kernel, nct=nct, rb=_pick_tile(ROW_BLOCK, tm), pending=pending)
    const = lambda shape: pl.BlockSpec(shape, lambda i: (0,) * len(shape),
                                       pipeline_mode=pl.Buffered(1))
    rope_idx = lambda i: (jnp.where(i < nct, 0, 1 + (i - nct) % per), 0)
    ctx_idx = lambda i: (jnp.minimum(i, nct - 1), 0)
    return pl.pallas_call(
        kern,
        out_shape=(jax.ShapeDtypeStruct((n, H_C * QK_PAD), BF16),
                   jax.ShapeDtypeStruct((n, H_C * QK_PAD), BF16),
                   jax.ShapeDtypeStruct((n, H_C * V_DIM), BF16),
                   jax.ShapeDtypeStruct((nc, KV_LORA), F32),
                   jax.ShapeDtypeStruct((nc, ROPE), F32)),
        grid=(n // tm,),
        in_specs=[pl.BlockSpec((tm, d), lambda i: (i, 0))] * (2 if pending else 1) + [
            pl.BlockSpec((None, 6, d), lambda i: (_group(i, nct, per), 0, 0)),
            const((1, d)), const(wd.shape), const((1, Q_LORA)), const((1, KV_LORA)),
            const(wq.shape), const((1, QK_PAD)), const(wkv.shape),
            const((1, NOPE)), const((1, LANES)),
            pl.BlockSpec((tm, LANES), rope_idx),
            pl.BlockSpec((tm, LANES), rope_idx),
        ],
        out_specs=(
            pl.BlockSpec((tm, H_C * QK_PAD), lambda i: (i, 0)),
            pl.BlockSpec((tm, H_C * QK_PAD), lambda i: (i, 0)),
            pl.BlockSpec((tm, H_C * V_DIM), lambda i: (i, 0)),
            pl.BlockSpec((tm, KV_LORA), ctx_idx),
            pl.BlockSpec((tm, ROPE), ctx_idx),
        ),
        compiler_params=_params("arbitrary"),
        name="inproj_c",
    )(*([x] + ([dx] if pending else []) + [mods, g1, wd, gcq, gckv, wq, gq, wkv, gkn, gkr, rope_c, rope_s]))


def _cache_kv_kernel(ckv_ref, krp_ref, wkv_ref, gkn_ref, gkr_ref, c_ref, s_ref, k_ref, v_ref):
    _mla_kv_expand(ckv_ref[...].astype(BF16), krp_ref[...], wkv_ref, gkn_ref, gkr_ref,
                   c_ref[...], s_ref[...], k_ref, v_ref)


def _cache_kv(ckv, krp, wkv, gkn, gkr, rope_c, rope_s, *, tm):
    n = ckv.shape[0]
    const = lambda shape: pl.BlockSpec(shape, lambda i: (0,) * len(shape))
    return pl.pallas_call(
        _cache_kv_kernel,
        out_shape=(jax.ShapeDtypeStruct((n, H_C * QK_PAD), BF16),
                   jax.ShapeDtypeStruct((n, H_C * V_DIM), BF16)),
        grid=(n // tm,),
        in_specs=[
            pl.BlockSpec((tm, KV_LORA), lambda i: (i, 0)),
            pl.BlockSpec((tm, LANES), lambda i: (i, 0)),
            const(wkv.shape), const((1, NOPE)), const((1, LANES)),
            const((tm, LANES)), const((tm, LANES)),
        ],
        out_specs=(pl.BlockSpec((tm, H_C * QK_PAD), lambda i: (i, 0)),
                   pl.BlockSpec((tm, H_C * V_DIM), lambda i: (i, 0))),
        compiler_params=_params("arbitrary"),
        name="cache_kv",
    )(ckv, krp, wkv, gkn, gkr, rope_c, rope_s)


def _mla_attn_kernel(q_ref, k_ref, v_ref, kx_ref, vx_ref, o_ref, kall_ref, vt_ref, *, tq):
    past = kx_ref.shape[0]
    kall_ref[0:past, :] = kx_ref[...]
    kall_ref[past:, :] = k_ref[...]
    vt_ref[:, 0:past] = vx_ref[...].astype(F32).T.astype(BF16)
    vt_ref[:, past:] = v_ref[...].astype(F32).T.astype(BF16)

    def scores(t):
        q = q_ref[t * tq:(t + 1) * tq, :]
        return _dot_nt(kall_ref[...], q)

    nblk = q_ref.shape[0] // tq
    s_next = scores(0)
    for t in range(nblk):
        s = s_next
        if t + 1 < nblk:
            s_next = scores(t + 1)
        m = jnp.max(s, axis=0, keepdims=True)
        p = jnp.exp2(s - m)
        l = jnp.sum(p, axis=0, keepdims=True)
        ot = _dot(vt_ref[...], p.astype(BF16)) / l
        o_ref[t * tq:(t + 1) * tq, :] = ot.T.astype(BF16)


def _mla_attn(q, k, v, kx, vx, *, nc, dseq, dbatch, past, tq):
    off = nc // dseq
    kern = functools.partial(_mla_attn_kernel, tq=tq)
    return pl.pallas_call(
        kern,
        out_shape=jax.ShapeDtypeStruct((dbatch * dseq, H_C * V_DIM), BF16),
        grid=(dbatch, H_C),
        in_specs=[
            pl.BlockSpec((dseq, QK_PAD), lambda b, h: (off + b, h)),
            pl.BlockSpec((dseq, QK_PAD), lambda b, h: (off + b, h)),
            pl.BlockSpec((dseq, V_DIM), lambda b, h: (off + b, h)),
            pl.BlockSpec((past, QK_PAD), lambda b, h: (b, h)),
            pl.BlockSpec((past, V_DIM), lambda b, h: (b, h)),
        ],
        out_specs=pl.BlockSpec((dseq, V_DIM), lambda b, h: (b, h)),
        scratch_shapes=[pltpu.VMEM((past + dseq, QK_PAD), BF16),
                        pltpu.VMEM((V_DIM, past + dseq), BF16)],
        compiler_params=_params("arbitrary", "arbitrary"),
        name="mla_attn",
    )(q, k, v, kx, vx)


def _rope_tables(dseq, tm):
    t = np.arange(dseq)
    quarter = ROPE // 4
    freqs = jnp.asarray(ROPE_BASE, F32) ** (-jnp.arange(quarter, dtype=F32) / quarter)
    ang_r = jnp.asarray(t // GRID_W, F32)[:, None] * freqs
    ang_c = jnp.asarray(t % GRID_W, F32)[:, None] * freqs
    zeros = jnp.zeros((dseq, LANES - ROPE), F32)
    cos = jnp.concatenate([jnp.cos(ang_r), jnp.cos(ang_r), jnp.cos(ang_c), jnp.cos(ang_c), zeros], axis=1)
    sin = jnp.concatenate([-jnp.sin(ang_r), jnp.sin(ang_r), -jnp.sin(ang_c), jnp.sin(ang_c), zeros], axis=1)
    ident = jnp.concatenate([jnp.ones((tm, ROPE), F32), jnp.zeros((tm, LANES - ROPE), F32)], axis=1)
    return (jnp.concatenate([ident, cos], axis=0),
            jnp.concatenate([jnp.zeros((tm, LANES), F32), sin], axis=0))


def _pick_tile(pref, *lengths):
    t = pref
    while any(l % t for l in lengths):
        t //= 2
    return t


def kernel(x_prompt, x_sample, cache_nat_k, cache_nat_v, cache_mla_ckv, cache_mla_krope, c, c_ctx,
           norm1_g, norm2_g, w_ada, b_ada, w_in_ab, g_qn_a, g_kn_a, rpb_a, conv_b_w, w_out_ab,
           w_down_c, g_cq, g_ckv, w_uq_c, w_ukv_c, g_qn_c, g_kn_c, w_o_c, w_ffn_in, w_ffn_out):
    batch, seq, d = x_prompt.shape
    dbatch, dseq, _ = x_sample.shape
    depth = w_ada.shape[0]
    n_even = w_in_ab.shape[0]
    past = cache_nat_k.shape[3]
    nc, ns = batch * seq, dbatch * dseq
    n = nc + ns
    dff = w_ffn_out.shape[1]
    assert nc % dseq == 0 and dseq % seq == 0 and dseq % GRID_W == 0

    tm = _pick_tile(512, nc, dseq)
    tm_c = _pick_tile(512, nc, dseq)
    tm_f = _pick_tile(1024, nc, dseq)
    tf = _pick_tile(512, dff)
    assert tm % seq == 0

    x = jnp.concatenate([x_prompt.reshape(nc, d), x_sample.reshape(ns, d)], axis=0)

    groups = 1 + dbatch
    gp = -(-groups // SUBLANES) * SUBLANES
    cond = jnp.concatenate([c_ctx[None, :], c, jnp.zeros((gp - groups, d), F32)], axis=0)
    mods_all = _adaln(cond, w_ada, b_ada).reshape(depth, gp, 6, d)

    rope_c, rope_s = _rope_tables(dseq, tm_c)
    ident_c = jnp.concatenate([jnp.ones((past, ROPE), F32), jnp.zeros((past, LANES - ROPE), F32)], axis=1)
    shape_kw = dict(nc=nc, dseq=dseq)

    w_in_ab_b = w_in_ab.astype(BF16)
    w_out_ab_b = w_out_ab.astype(BF16)
    w_o_c_b = w_o_c.astype(BF16)
    w_ffn_in_b = w_ffn_in.astype(BF16)
    w_ffn_out_b = w_ffn_out.astype(BF16)

    nat = None
    mla_ckv, mla_kr = [], []
    dx = None
    for l in range(depth):
        i = l // 2
        mods = mods_all[l]
        if l % 2 == 0:
            gq = (g_qn_a[i] * (DH_A ** -0.5 * LOG2E))[None, :]
            qkv, gb, z, nk, nv = _inproj_ab(
                x, dx, mods, norm1_g[l][None, :], w_in_ab_b, gq, g_kn_a[i][None, :], nat, i,
                seq=seq, batch=batch, n_even=n_even, tm=tm, **shape_kw)
            nat = (nk, nv)
            ya_c = _ctx_attn(qkv, qkv, qkv, (0, 1, 2), nc=nc, batch=batch, seq=seq,
                             heads=H_A, dq=DH_A, dv=DH_A)
            ya_s = _nat_attn(rpb_a[i], qkv, cache_nat_k, cache_nat_v, i, dbatch=dbatch, **shape_kw)
            x, h2 = _outproj(x, dx, mods, norm2_g[l][None, :], ya_c, ya_s, w_out_ab_b, i,
                             (gb, z, conv_b_w[i]), seq=seq, tm=tm, **shape_kw)
        else:
            wd = jnp.pad(w_down_c[i], ((0, 0), (0, LANES - ROPE))).astype(BF16)
            wq = jnp.pad(w_uq_c[i].reshape(Q_LORA, H_C, QK_DIM),
                         ((0, 0), (0, 0), (0, QK_PAD - QK_DIM))).reshape(Q_LORA, H_C * QK_PAD).astype(BF16)
            wkv = w_ukv_c[i].astype(BF16)
            gq = jnp.pad(g_qn_c[i] * (QK_DIM ** -0.5 * LOG2E), (0, QK_PAD - QK_DIM))[None, :]
            gkn = g_kn_c[i][None, :NOPE]
            gkr = jnp.pad(g_kn_c[i][NOPE:], (0, LANES - ROPE))[None, :]
            q, k, v, ckv_n, kr = _inproj_c(
                x, dx, mods, norm1_g[l][None, :], wd, g_cq[i][None, :], g_ckv[i][None, :],
                wq, gq, wkv, gkn, gkr, rope_c, rope_s, tm=tm_c, **shape_kw)
            mla_ckv.append(ckv_n.reshape(batch, seq, KV_LORA))
            mla_kr.append(kr.reshape(batch, seq, ROPE))
            kx, vx = _cache_kv(
                cache_mla_ckv[:, i].reshape(dbatch * past, KV_LORA),
                jnp.pad(cache_mla_krope[:, i].reshape(dbatch * past, ROPE), ((0, 0), (0, LANES - ROPE))),
                wkv, gkn, gkr, ident_c, jnp.zeros((past, LANES), F32), tm=past)
            ya_c = _ctx_attn(q, k, v, (0, 0, 0), nc=nc, batch=batch, seq=seq,
                             heads=H_C, dq=QK_PAD, dv=V_DIM)
            ya_s = _mla_attn(q, k, v, kx, vx, dbatch=dbatch, past=past,
                             tq=_pick_tile(256, dseq), **shape_kw)
            x, h2 = _outproj(x, dx, mods, norm2_g[l][None, :], ya_c, ya_s, w_o_c_b, i,
                             seq=seq, tm=tm, **shape_kw)
        last = l == depth - 1
        dx = _ffn(x, h2, mods, w_ffn_in_b, w_ffn_out_b, l, last=last,
                  tm=tm if last else tm_f, tf=tf, **shape_kw)

    return (dx[0].reshape(batch, seq, d), dx[1].reshape(dbatch, dseq, d), nat[0], nat[1],
            jnp.stack(mla_ckv, axis=1), jnp.stack(mla_kr, axis=1))
```

```python
import functools

import numpy as np
import jax
import jax.numpy as jnp
from jax import lax
from jax.experimental import pallas as pl
from jax.experimental.pallas import tpu as pltpu

F32 = jnp.float32
BF16 = jnp.bfloat16

EPS = 1e-6
GRID_W = 64
WIN_R = 8
WIN_C = 16
ROPE_BASE = 10000.0
H_A = 8
DH_A = 128
H_C = 16
Q_LORA = 512
KV_LORA = 256
NOPE = 128
ROPE = 64
V_DIM = 128
QK_DIM = NOPE + ROPE
QK_PAD = 256
LANES = 128
SUBLANES = 8
NEG = -1e30
LOG2E = 1.4426950408889634

VMEM_LIMIT = 56 * 1024 * 1024
ROW_BLOCK = 128
ROW_BLOCK_AB = 256


def _params(*sem):
    return pltpu.CompilerParams(dimension_semantics=sem, vmem_limit_bytes=VMEM_LIMIT)


def _group(i, nct, per):
    return jnp.where(i < nct, 0, 1 + (i - nct) // per)


def _modulated(x, g, shift, scale):
    ms = jnp.mean(x * x, axis=-1, keepdims=True)
    return (x * lax.rsqrt(ms + EPS) * g) * (1.0 + scale) + shift


def _dot(a, b):
    return jnp.dot(a, b, preferred_element_type=F32)


def _dot_nt(a, b):
    return lax.dot_general(a, b, (((1,), (1,)), ((), ())), preferred_element_type=F32)


def _ada_kernel(c_ref, w_ref, b_ref, o_ref):
    c = c_ref[...]
    s = (c / (1.0 + jnp.exp(-c))).astype(BF16)
    o_ref[...] = _dot(s, w_ref[...].astype(BF16)) + b_ref[...]


def _adaln(cond, w_ada, b_ada):
    depth, d, n6 = w_ada.shape
    gp = cond.shape[0]
    tn = 1024 if n6 % 1024 == 0 else n6
    return pl.pallas_call(
        _ada_kernel,
        out_shape=jax.ShapeDtypeStruct((depth, gp, n6), F32),
        grid=(depth, n6 // tn),
        in_specs=[
            pl.BlockSpec((gp, d), lambda l, n: (0, 0)),
            pl.BlockSpec((None, d, tn), lambda l, n: (l, 0, n)),
            pl.BlockSpec((None, 1, tn), lambda l, n: (l, 0, n)),
        ],
        out_specs=pl.BlockSpec((None, gp, tn), lambda l, n: (l, 0, n)),
        compiler_params=_params("arbitrary", "arbitrary"),
        name="adaln",
    )(cond, w_ada, b_ada.reshape(depth, 1, n6))


def _row_spec(tm, d, nct, second, which):
    if second != "select":
        return pl.BlockSpec((tm, d), lambda i, *_: (i, 0))
    if which == 0:
        return pl.BlockSpec((tm, d), lambda i, *_: (jnp.minimum(i, nct - 1), 0))
    return pl.BlockSpec((tm, d), lambda i, *_: (jnp.maximum(i - nct, 0), 0))


def _residual(x_ref, x2_ref, rows, second, is_ctx):
    x = x_ref[rows, :]
    if second == "add":
        x = x + x2_ref[rows, :]
    elif second == "select":
        x = jnp.where(is_ctx, x, x2_ref[rows, :])
    return x


def _inproj_ab_kernel(x_ref, *rest, nct, seq, first, rb, second):
    x2_ref = None
    if second:
        x2_ref, *rest = rest
    mod_ref, g1_ref, w_ref, gq_ref, gk_ref, *rest = rest
    if first:
        qkv_ref, gb_ref, z_ref, natk_ref, natv_ref = rest
    else:
        _, _, qkv_ref, gb_ref, z_ref, natk_ref, natv_ref = rest
    i = pl.program_id(0)
    tm = x_ref.shape[0]
    na = H_A * DH_A
    is_ctx = i < nct
    gs = g1_ref[...] * (1.0 + mod_ref[1:2, :])

    def nat_store(nat_ref, rows, hh, val):
        bb, off = rows.start // seq, rows.start % seq
        if first:
            nat_ref[bb, 0, hh, off:off + rb, :] = val
        else:
            nat_ref[bb, hh, off:off + rb, :] = val

    def head_norm(rows, r, col, gain_ref, nat_ref):
        for hh in range(H_A):
            t = r[:, hh * DH_A:(hh + 1) * DH_A]
            ms = jnp.mean(t * t, axis=-1, keepdims=True)
            val = t * lax.rsqrt(ms + EPS) * gain_ref[...]
            qkv_ref[rows, col * na + hh * DH_A:col * na + (hh + 1) * DH_A] = val.astype(BF16)
            if nat_ref is not None:
                nat_store(nat_ref, rows, hh, val)

    def tile(ctx):
        if ctx and first and natk_ref.shape[1] > 1:
            natk_ref[:, 1:] = jnp.zeros_like(natk_ref[:, 1:])
            natv_ref[:, 1:] = jnp.zeros_like(natv_ref[:, 1:])
        for r0 in range(0, tm, rb):
            rows = slice(r0, r0 + rb)
            x = _residual(x_ref, x2_ref, rows, second, is_ctx)
            ms = jnp.mean(x * x, axis=-1, keepdims=True)
            h = (x * lax.rsqrt(ms + EPS) * gs + mod_ref[0:1, :]).astype(BF16)

            def proj(col):
                return _dot(h, w_ref[:, col * na:(col + 1) * na])

            head_norm(rows, proj(0), 0, gq_ref, None)
            head_norm(rows, proj(1), 1, gk_ref, natk_ref if ctx else None)
            v = proj(2)
            qkv_ref[rows, 2 * na:3 * na] = v.astype(BF16)
            if ctx:
                for hh in range(H_A):
                    nat_store(natv_ref, rows, hh, v[:, hh * DH_A:(hh + 1) * DH_A])
            gb_ref[rows, :] = proj(3)
            gc = proj(4)
            z_ref[rows, :] = gc * proj(5)

    pl.when(is_ctx)(lambda: tile(True))
    pl.when(jnp.logical_not(is_ctx))(lambda: tile(False))


def _inproj_ab(x, x2, second, mods, g1, w_in, gq, gk, nat_prev, slot,
               *, nc, seq, dseq, batch, n_even, tm):
    d = x.shape[1]
    n = nc + (x2.shape[0] if second == "select" else x.shape[0] - nc)
    na = H_A * DH_A
    nct = nc // tm
    per = dseq // tm
    first = nat_prev is None
    kern = functools.partial(_inproj_ab_kernel, nct=nct, seq=seq, first=first,
                             rb=_pick_tile(ROW_BLOCK_AB, tm, seq), second=second)
    nat_shape = jax.ShapeDtypeStruct((batch, n_even, H_A, seq, DH_A), F32)
    if first:
        nat_spec = pl.BlockSpec((tm // seq, n_even, H_A, seq, DH_A),
                                lambda i: (jnp.minimum(i, nct - 1), 0, 0, 0, 0))
    else:
        nat_spec = pl.BlockSpec((tm // seq, None, H_A, seq, DH_A),
                                lambda i: (jnp.minimum(i, nct - 1), slot, 0, 0, 0))
    in_specs = [_row_spec(tm, d, nct, second, 0)] + ([_row_spec(tm, d, nct, second, 1)] if second else []) + [
        pl.BlockSpec((None, 6, d), lambda i: (_group(i, nct, per), 0, 0)),
        pl.BlockSpec((1, d), lambda i: (0, 0)),
        pl.BlockSpec((None,) + w_in.shape[1:], lambda i: (slot, 0, 0), pipeline_mode=pl.Buffered(1)),
        pl.BlockSpec((1, DH_A), lambda i: (0, 0)),
        pl.BlockSpec((1, DH_A), lambda i: (0, 0)),
    ]
    args = [x] + ([x2] if second else []) + [mods, g1, w_in, gq, gk]
    aliases = {}
    if not first:
        in_specs += [pl.BlockSpec(memory_space=pl.ANY)] * 2
        aliases = {len(args): 3, len(args) + 1: 4}
        args += list(nat_prev)
    return pl.pallas_call(
        kern,
        out_shape=(jax.ShapeDtypeStruct((n, 3 * na), BF16),
                   jax.ShapeDtypeStruct((n, na), F32),
                   jax.ShapeDtypeStruct((n, na), F32),
                   nat_shape, nat_shape),
        grid=(n // tm,),
        in_specs=in_specs,
        out_specs=(
            pl.BlockSpec((tm, 3 * na), lambda i: (i, 0)),
            pl.BlockSpec((tm, na), lambda i: (i, 0)),
            pl.BlockSpec((tm, na), lambda i: (i, 0)),
            nat_spec, nat_spec,
        ),
        input_output_aliases=aliases,
        compiler_params=_params("arbitrary"),
        name="inproj_ab",
    )(*args)


def _ctx_attn_kernel(q_ref, k_ref, v_ref, o_ref, *, heads, dq, dv):
    def scores(hh):
        return _dot_nt(q_ref[:, hh * dq:(hh + 1) * dq], k_ref[:, hh * dq:(hh + 1) * dq])

    s_next = scores(0)
    for hh in range(heads):
        s = s_next
        if hh + 1 < heads:
            s_next = scores(hh + 1)
        m = jnp.max(s, axis=-1, keepdims=True)
        p = jnp.exp2(s - m)
        l = jnp.sum(p, axis=-1, keepdims=True)
        o = _dot(p.astype(BF16), v_ref[:, hh * dv:(hh + 1) * dv]) / l
        o_ref[:, hh * dv:(hh + 1) * dv] = o.astype(BF16)


def _ctx_attn(q_arr, k_arr, v_arr, cols, *, nc, batch, seq, heads, dq, dv):
    cq, ck, cv = cols
    kern = functools.partial(_ctx_attn_kernel, heads=heads, dq=dq, dv=dv)
    return pl.pallas_call(
        kern,
        out_shape=jax.ShapeDtypeStruct((nc, heads * dv), BF16),
        grid=(batch,),
        in_specs=[
            pl.BlockSpec((seq, heads * dq), lambda b: (b, cq)),
            pl.BlockSpec((seq, heads * dq), lambda b: (b, ck)),
            pl.BlockSpec((seq, heads * dv), lambda b: (b, cv)),
        ],
        out_specs=pl.BlockSpec((seq, heads * dv), lambda b: (b, 0)),
        compiler_params=_params("arbitrary"),
        name="ctx_attn",
    )(q_arr, k_arr, v_arr)


NAT_QR = 8
NAT_KR = 16


def _nat_window(a, rows):
    return int(np.clip(a * NAT_QR - WIN_R // 2, 0, rows - NAT_KR))


def _nat_attn_kernel(rpb_ref, q_ref, k_ref, v_ref, ck_ref, cv_ref, o_ref,
                     t2_ref, bias_ref, *, rows):
    hh = pl.program_id(0)
    b = pl.program_id(1)
    nblk = rows // NAT_QR
    n_dr = 2 * WIN_R - 1
    n_dc = 2 * WIN_C - 1

    @pl.when(b == 0)
    def _():
        j = lax.broadcasted_iota(jnp.int32, (GRID_W, LANES), 0)
        kc = lax.broadcasted_iota(jnp.int32, (GRID_W, LANES), 1) % GRID_W
        cs = jnp.clip(j - WIN_C // 2, 0, GRID_W - WIN_C)
        col_in = (kc >= cs) & (kc < cs + WIN_C)
        dc = kc - j + (WIN_C - 1)
        for d in range(n_dr):
            t = jnp.full((GRID_W, LANES), NEG, F32)
            for c in range(n_dc):
                t = jnp.where(col_in & (dc == c), rpb_ref[hh, d * n_dc + c] * LOG2E, t)
            t2_ref[d] = t
        lane = lax.broadcasted_iota(jnp.int32, (GRID_W, LANES), 1)
        negt = jnp.full((GRID_W, LANES), NEG, F32)
        for a in range(nblk):
            ws = _nat_window(a, rows)
            for qr in range(NAT_QR):
                r = a * NAT_QR + qr
                rs = int(np.clip(r - WIN_R // 2, 0, rows - WIN_R))
                for pp in range(NAT_KR // 2):
                    halves = []
                    for kk in (2 * pp, 2 * pp + 1):
                        kr = ws + kk
                        halves.append(t2_ref[kr - r + WIN_R - 1] if rs <= kr < rs + WIN_R else None)
                    if halves[0] is None and halves[1] is None:
                        blk = negt
                    else:
                        left = negt if halves[0] is None else halves[0]
                        right = negt if halves[1] is None else halves[1]
                        blk = jnp.where(lane < GRID_W, left, right)
                    bias_ref[a, qr * GRID_W:(qr + 1) * GRID_W, pp * LANES:(pp + 1) * LANES] = blk

    ck = ck_ref[...].astype(BF16)
    cv = cv_ref[...].astype(BF16)
    tq = NAT_QR * GRID_W
    tk = NAT_KR * GRID_W

    def scores(a):
        ws = _nat_window(a, rows)
        q = q_ref[a * tq:(a + 1) * tq, :]
        kw = k_ref[ws * GRID_W:ws * GRID_W + tk, :]
        return _dot_nt(q, kw) + bias_ref[a], _dot_nt(q, ck)

    s_next = scores(0)
    for a in range(nblk):
        s_loc, s_ctx = s_next
        if a + 1 < nblk:
            s_next = scores(a + 1)
        ws = _nat_window(a, rows)
        vw = v_ref[ws * GRID_W:ws * GRID_W + tk, :]
        m = jnp.maximum(jnp.max(s_loc, axis=-1, keepdims=True),
                        jnp.max(s_ctx, axis=-1, keepdims=True))
        p_loc = jnp.exp2(s_loc - m)
        p_ctx = jnp.exp2(s_ctx - m)
        l = jnp.sum(p_loc, axis=-1, keepdims=True) + jnp.sum(p_ctx, axis=-1, keepdims=True)
        o = (_dot(p_loc.astype(BF16), vw) + _dot(p_ctx.astype(BF16), cv)) / l
        o_ref[a * tq:(a + 1) * tq, :] = o.astype(BF16)


def _nat_attn(rpb, qkv, cache_k, cache_v, slot, *, nc, dseq, dbatch):
    rows = dseq // GRID_W
    assert rows % NAT_QR == 0 and rows >= NAT_KR
    nblk = rows // NAT_QR
    past = cache_k.shape[3]
    off = nc // dseq
    n_dr, n_dc = 2 * WIN_R - 1, 2 * WIN_C - 1
    kern = functools.partial(_nat_attn_kernel, rows=rows)
    cache_spec = pl.BlockSpec((None, None, None, past, DH_A), lambda h, b: (b, slot, h, 0, 0))
    return pl.pallas_call(
        kern,
        out_shape=jax.ShapeDtypeStruct((dbatch * dseq, H_A * DH_A), BF16),
        grid=(H_A, dbatch),
        in_specs=[
            pl.BlockSpec(memory_space=pltpu.SMEM),
            pl.BlockSpec((dseq, DH_A), lambda h, b: (off + b, h)),
            pl.BlockSpec((dseq, DH_A), lambda h, b: (off + b, H_A + h)),
            pl.BlockSpec((dseq, DH_A), lambda h, b: (off + b, 2 * H_A + h)),
            cache_spec, cache_spec,
        ],
        out_specs=pl.BlockSpec((dseq, DH_A), lambda h, b: (b, h)),
        scratch_shapes=[pltpu.VMEM((n_dr, GRID_W, LANES), F32),
                        pltpu.VMEM((nblk, NAT_QR * GRID_W, NAT_KR * GRID_W), F32)],
        compiler_params=_params("arbitrary", "arbitrary"),
        name="nat_attn",
    )(rpb.reshape(H_A, n_dr * n_dc), qkv, qkv, qkv, cache_k, cache_v)


def _outproj_kernel(x_ref, *refs, conv, nct, seq, dseq, rb, second):
    x2_ref = None
    if second:
        x2_ref, *refs = refs
    if conv:
        (mod_ref, g2_ref, yac_ref, yas_ref, gb_ref, z_ref, zp_ref, zn_ref, cw_ref, w_ref,
         o_ref, h2_ref, zs_ref) = refs
    else:
        mod_ref, g2_ref, yac_ref, yas_ref, w_ref, o_ref, h2_ref = refs
    i = pl.program_id(0)
    tm = x_ref.shape[0]
    ka = yac_ref.shape[1]
    if conv:
        h8 = SUBLANES
        zs_ref[0:h8, :] = zp_ref[...]
        zs_ref[h8:h8 + tm, :] = z_ref[...]
        zs_ref[h8 + tm:2 * h8 + tm, :] = zn_ref[...]
        last = jnp.where(i < nct, seq - 1, dseq - 1)
    gate = mod_ref[2:3, :]
    gs = g2_ref[...] * (1.0 + mod_ref[4:5, :])
    shift = mod_ref[3:4, :]
    for r0 in range(0, tm, rb):
        rows = slice(r0, r0 + rb)
        ya = jnp.where(i < nct, yac_ref[rows, :], yas_ref[rows, :])
        y = _dot(ya, w_ref[0:ka, :])
        if conv:
            row = r0 + lax.broadcasted_iota(jnp.int32, (rb, 1), 0)
            pos = jnp.where(i < nct, row % seq, ((i - nct) * tm + row) % dseq)
            z_prev = jnp.where(pos == 0, 0.0, zs_ref[h8 - 1 + r0:h8 - 1 + r0 + rb, :])
            z_next = jnp.where(pos == last, 0.0, zs_ref[h8 + 1 + r0:h8 + 1 + r0 + rb, :])
            cv = cw_ref[0:1, :] * z_prev + cw_ref[1:2, :] * z_ref[rows, :] + cw_ref[2:3, :] * z_next
            yb = gb_ref[rows, :] * cv
            y = y + _dot(yb.astype(BF16), w_ref[ka:, :])
        xn = _residual(x_ref, x2_ref, rows, second, i < nct) + gate * y
        o_ref[rows, :] = xn
        ms = jnp.mean(xn * xn, axis=-1, keepdims=True)
        h2_ref[rows, :] = (xn * lax.rsqrt(ms + EPS) * gs + shift).astype(BF16)


def _outproj(x, x2, second, mods, g2, ya_c, ya_s, w_out, slot, conv_in=None, *, nc, seq, dseq, tm):
    d = x.shape[1]
    n = nc + ya_s.shape[0]
    ka = ya_c.shape[1]
    nct = nc // tm
    per = dseq // tm
    conv = conv_in is not None
    kern = functools.partial(_outproj_kernel, conv=conv, nct=nct, seq=seq, dseq=dseq,
                             rb=_pick_tile(ROW_BLOCK, tm), second=second)
    in_specs = [_row_spec(tm, d, nct, second, 0)] + ([_row_spec(tm, d, nct, second, 1)] if second else []) + [
        pl.BlockSpec((None, 6, d), lambda i: (_group(i, nct, per), 0, 0)),
        pl.BlockSpec((1, d), lambda i: (0, 0)),
        pl.BlockSpec((tm, ka), lambda i: (jnp.minimum(i, nct - 1), 0)),
        pl.BlockSpec((tm, ka), lambda i: (jnp.maximum(i - nct, 0), 0)),
    ]
    args = [x] + ([x2] if second else []) + [mods, g2, ya_c, ya_s]
    scratch = []
    if conv:
        gb, z, cw = conv_in
        nb = z.shape[1]
        t8 = tm // SUBLANES
        n8 = n // SUBLANES
        in_specs += [
            pl.BlockSpec((tm, nb), lambda i: (i, 0)),
            pl.BlockSpec((tm, nb), lambda i: (i, 0)),
            pl.BlockSpec((SUBLANES, nb), lambda i: (jnp.maximum(i * t8 - 1, 0), 0)),
            pl.BlockSpec((SUBLANES, nb), lambda i: (jnp.minimum((i + 1) * t8, n8 - 1), 0)),
            pl.BlockSpec((3, nb), lambda i: (0, 0)),
        ]
        args += [gb, z, z, z, cw]
        scratch = [pltpu.VMEM((tm + 2 * SUBLANES, nb), F32)]
    in_specs.append(pl.BlockSpec((None,) + w_out.shape[1:], lambda i: (slot, 0, 0),
                                 pipeline_mode=pl.Buffered(1)))
    args.append(w_out)
    return pl.pallas_call(
        kern,
        out_shape=(jax.ShapeDtypeStruct((n, d), F32), jax.ShapeDtypeStruct((n, d), BF16)),
        grid=(n // tm,),
        in_specs=in_specs,
        out_specs=(pl.BlockSpec((tm, d), lambda i: (i, 0)), pl.BlockSpec((tm, d), lambda i: (i, 0))),
        scratch_shapes=scratch,
        compiler_params=_params("arbitrary"),
        name="outproj_conv" if conv else "outproj",
    )(*args)


def _ffn_kernel(*refs, last, nct):
    if last:
        x_ref, h_ref, mod_ref, wg_ref, wu_ref, wo_ref, oc_ref, os_ref, acc_ref = refs
    else:
        h_ref, mod_ref, wg_ref, wu_ref, wo_ref, acc_ref = refs
    i = pl.program_id(0)
    f = pl.program_id(1)

    def step(start):
        h = h_ref[...]
        g = _dot(h, wg_ref[...])
        u = _dot(h, wu_ref[...])
        a = (g / (1.0 + jnp.exp(-g))) * u
        part = _dot(a.astype(BF16), wo_ref[...])
        acc_ref[...] = part if start else acc_ref[...] + part

    pl.when(f == 0)(lambda: step(True))
    pl.when(f != 0)(lambda: step(False))

    @pl.when(f == pl.num_programs(1) - 1)
    def _():
        delta = mod_ref[5:6, :] * acc_ref[...]
        if last:
            @pl.when(i < nct)
            def _():
                oc_ref[...] = x_ref[...] + delta

            @pl.when(i >= nct)
            def _():
                os_ref[...] = x_ref[...] + delta
        else:
            acc_ref[...] = delta


def _ffn(x, h2, mods, w_in, w_out, layer, *, last, nc, dseq, tm, tf):
    n, d = h2.shape
    dff = w_out.shape[1]
    nf = dff // tf
    nct = nc // tm
    per = dseq // tm
    kern = functools.partial(_ffn_kernel, last=last, nct=nct)
    row_spec = pl.BlockSpec((tm, d), lambda i, f: (i, 0))
    in_specs = [
        row_spec,
        pl.BlockSpec((None, 6, d), lambda i, f: (_group(i, nct, per), 0, 0)),
        pl.BlockSpec((None, d, tf), lambda i, f: (layer, 0, f)),
        pl.BlockSpec((None, d, tf), lambda i, f: (layer, 0, nf + f)),
        pl.BlockSpec((None, tf, d), lambda i, f: (layer, f, 0)),
    ]
    args = [h2, mods, w_in, w_in, w_out]
    if last:
        in_specs.insert(0, row_spec)
        args.insert(0, x)
        out_shape = (jax.ShapeDtypeStruct((nc, d), F32), jax.ShapeDtypeStruct((n - nc, d), F32))
        out_specs = (pl.BlockSpec((tm, d), lambda i, f: (jnp.minimum(i, nct - 1), 0)),
                     pl.BlockSpec((tm, d), lambda i, f: (jnp.maximum(i - nct, 0), 0)))
        scratch = [pltpu.VMEM((tm, d), F32)]
    else:
        out_shape = jax.ShapeDtypeStruct((n, d), F32)
        out_specs = row_spec
        scratch = []
    return pl.pallas_call(
        kern,
        out_shape=out_shape,
        grid=(n // tm, nf),
        in_specs=in_specs,
        out_specs=out_specs,
        scratch_shapes=scratch,
        compiler_params=_params("arbitrary", "arbitrary"),
        name="ffn",
    )(*args)


def _rope(x, c, s):
    lane = lax.broadcasted_iota(jnp.int32, x.shape, 1)
    quarter = ROPE // 4
    swapped = jnp.where(lane % (2 * quarter) < quarter,
                        pltpu.roll(x, LANES - quarter, 1), pltpu.roll(x, quarter, 1))
    return x * c + swapped * s


def _mla_kv_expand(ckv_bf16, krp, wkv_ref, gkn_ref, gkr_ref, c, s, k_ref, v_ref, rows=slice(None)):
    kv = _dot(ckv_bf16, wkv_ref[...])
    krr = _rope(krp * gkr_ref[...], c, s)
    ss_kr = jnp.sum(krp * krp, axis=-1, keepdims=True)
    for hh in range(H_C):
        base = hh * (NOPE + V_DIM)
        kn = kv[:, base:base + NOPE]
        rstd = lax.rsqrt((jnp.sum(kn * kn, axis=-1, keepdims=True) + ss_kr) / QK_DIM + EPS)
        k_ref[rows, hh * QK_PAD:hh * QK_PAD + NOPE] = (kn * rstd * gkn_ref[...]).astype(BF16)
        k_ref[rows, hh * QK_PAD + NOPE:(hh + 1) * QK_PAD] = (krr * rstd).astype(BF16)
        v_ref[rows, hh * V_DIM:(hh + 1) * V_DIM] = kv[:, base + NOPE:base + NOPE + V_DIM].astype(BF16)


def _inproj_c_kernel(x_ref, *rest, nct, rb, second):
    x2_ref = None
    if second:
        x2_ref, *rest = rest
    (mod_ref, g1_ref, wd_ref, gcq_ref, gckv_ref, wq_ref, gq_ref, wkv_ref, gkn_ref, gkr_ref,
     c_ref, s_ref, q_ref, k_ref, v_ref, ckv_ref, kr_ref) = rest
    i = pl.program_id(0)
    tm = x_ref.shape[0]
    blocks = [slice(r * rb, (r + 1) * rb) for r in range(tm // rb)]
    gs = g1_ref[...] * (1.0 + mod_ref[1:2, :])

    def latents(rows):
        x = _residual(x_ref, x2_ref, rows, second, i < nct)
        ms = jnp.mean(x * x, axis=-1, keepdims=True)
        h = (x * lax.rsqrt(ms + EPS) * gs + mod_ref[0:1, :]).astype(BF16)
        dn = _dot(h, wd_ref[...])
        cq = dn[:, :Q_LORA]
        ckv = dn[:, Q_LORA:Q_LORA + KV_LORA]
        krp = dn[:, Q_LORA + KV_LORA:]
        cqn = cq * lax.rsqrt(jnp.mean(cq * cq, axis=-1, keepdims=True) + EPS) * gcq_ref[...]
        ckvn = ckv * lax.rsqrt(jnp.mean(ckv * ckv, axis=-1, keepdims=True) + EPS) * gckv_ref[...]
        return cqn.astype(BF16), ckvn, krp

    def expand(rows, cqn, ckvn, krp, ctx):
        if ctx:
            ckv_ref[rows, :] = ckvn
            kr_ref[rows, :] = krp[:, :ROPE]
        c = c_ref[rows, :]
        s = s_ref[rows, :]
        q = _dot(cqn, wq_ref[...])
        for hh in range(H_C):
            qh = q[:, hh * QK_PAD:(hh + 1) * QK_PAD]
            rstd = lax.rsqrt(jnp.sum(qh * qh, axis=-1, keepdims=True) / QK_DIM + EPS)
            qn = qh * rstd * gq_ref[...]
            q_ref[rows, hh * QK_PAD:hh * QK_PAD + NOPE] = qn[:, :NOPE].astype(BF16)
            q_ref[rows, hh * QK_PAD + NOPE:(hh + 1) * QK_PAD] = _rope(qn[:, NOPE:], c, s).astype(BF16)
        _mla_kv_expand(ckvn.astype(BF16), krp, wkv_ref, gkn_ref, gkr_ref, c, s, k_ref, v_ref, rows)

    def tile(ctx):
        nxt = latents(blocks[0])
        for r, rows in enumerate(blocks):
            cur = nxt
            if r + 1 < len(blocks):
                nxt = latents(blocks[r + 1])
            expand(rows, *cur, ctx)

    pl.when(i < nct)(lambda: tile(True))
    pl.when(i >= nct)(lambda: tile(False))


def _inproj_c(x, x2, second, mods, g1, wd, gcq, gckv, wq, gq, wkv, gkn, gkr, rope_c, rope_s,
              *, nc, dseq, tm):
    d = x.shape[1]
    n = nc + (x2.shape[0] if second == "select" else x.shape[0] - nc)
    nct = nc // tm
    per = dseq // tm
    kern = functools.partial(_inproj_c_kernel, nct=nct, rb=_pick_tile(ROW_BLOCK, tm), second=second)
    const = lambda shape: pl.BlockSpec(shape, lambda i: (0,) * len(shape),
                                       pipeline_mode=pl.Buffered(1))
    rope_idx = lambda i: (jnp.where(i < nct, 0, 1 + (i - nct) % per), 0)
    ctx_idx = lambda i: (jnp.minimum(i, nct - 1), 0)
    return pl.pallas_call(
        kern,
        out_shape=(jax.ShapeDtypeStruct((n, H_C * QK_PAD), BF16),
                   jax.ShapeDtypeStruct((n, H_C * QK_PAD), BF16),
                   jax.ShapeDtypeStruct((n, H_C * V_DIM), BF16),
                   jax.ShapeDtypeStruct((nc, KV_LORA), F32),
                   jax.ShapeDtypeStruct((nc, ROPE), F32)),
        grid=(n // tm,),
        in_specs=[_row_spec(tm, d, nct, second, 0)] + ([_row_spec(tm, d, nct, second, 1)] if second else []) + [
            pl.BlockSpec((None, 6, d), lambda i: (_group(i, nct, per), 0, 0)),
            const((1, d)), const(wd.shape), const((1, Q_LORA)), const((1, KV_LORA)),
            const(wq.shape), const((1, QK_PAD)), const(wkv.shape),
            const((1, NOPE)), const((1, LANES)),
            pl.BlockSpec((tm, LANES), rope_idx),
            pl.BlockSpec((tm, LANES), rope_idx),
        ],
        out_specs=(
            pl.BlockSpec((tm, H_C * QK_PAD), lambda i: (i, 0)),
            pl.BlockSpec((tm, H_C * QK_PAD), lambda i: (i, 0)),
            pl.BlockSpec((tm, H_C * V_DIM), lambda i: (i, 0)),
            pl.BlockSpec((tm, KV_LORA), ctx_idx),
            pl.BlockSpec((tm, ROPE), ctx_idx),
        ),
        compiler_params=_params("arbitrary"),
        name="inproj_c",
    )(*([x] + ([x2] if second else []) + [mods, g1, wd, gcq, gckv, wq, gq, wkv, gkn, gkr, rope_c, rope_s]))


def _cache_kv_kernel(ckv_ref, krp_ref, wkv_ref, gkn_ref, gkr_ref, c_ref, s_ref, k_ref, v_ref):
    _mla_kv_expand(ckv_ref[...].astype(BF16), krp_ref[...], wkv_ref, gkn_ref, gkr_ref,
                   c_ref[...], s_ref[...], k_ref, v_ref)


def _cache_kv(ckv, krp, wkv, gkn, gkr, rope_c, rope_s, *, tm):
    n = ckv.shape[0]
    const = lambda shape: pl.BlockSpec(shape, lambda i: (0,) * len(shape))
    return pl.pallas_call(
        _cache_kv_kernel,
        out_shape=(jax.ShapeDtypeStruct((n, H_C * QK_PAD), BF16),
                   jax.ShapeDtypeStruct((n, H_C * V_DIM), BF16)),
        grid=(n // tm,),
        in_specs=[
            pl.BlockSpec((tm, KV_LORA), lambda i: (i, 0)),
            pl.BlockSpec((tm, LANES), lambda i: (i, 0)),
            const(wkv.shape), const((1, NOPE)), const((1, LANES)),
            const((tm, LANES)), const((tm, LANES)),
        ],
        out_specs=(pl.BlockSpec((tm, H_C * QK_PAD), lambda i: (i, 0)),
                   pl.BlockSpec((tm, H_C * V_DIM), lambda i: (i, 0))),
        compiler_params=_params("arbitrary"),
        name="cache_kv",
    )(ckv, krp, wkv, gkn, gkr, rope_c, rope_s)


def _mla_attn_kernel(q_ref, k_ref, v_ref, kx_ref, vx_ref, o_ref, kall_ref, vt_ref, *, tq):
    past = kx_ref.shape[0]
    kall_ref[0:past, :] = kx_ref[...]
    kall_ref[past:, :] = k_ref[...]
    vt_ref[:, 0:past] = vx_ref[...].astype(F32).T.astype(BF16)
    vt_ref[:, past:] = v_ref[...].astype(F32).T.astype(BF16)

    def scores(t):
        q = q_ref[t * tq:(t + 1) * tq, :]
        return _dot_nt(kall_ref[...], q)

    nblk = q_ref.shape[0] // tq
    s_next = scores(0)
    for t in range(nblk):
        s = s_next
        if t + 1 < nblk:
            s_next = scores(t + 1)
        m = jnp.max(s, axis=0, keepdims=True)
        p = jnp.exp2(s - m)
        l = jnp.sum(p, axis=0, keepdims=True)
        ot = _dot(vt_ref[...], p.astype(BF16)) / l
        o_ref[t * tq:(t + 1) * tq, :] = ot.T.astype(BF16)


def _mla_attn(q, k, v, kx, vx, *, nc, dseq, dbatch, past, tq):
    off = nc // dseq
    kern = functools.partial(_mla_attn_kernel, tq=tq)
    return pl.pallas_call(
        kern,
        out_shape=jax.ShapeDtypeStruct((dbatch * dseq, H_C * V_DIM), BF16),
        grid=(dbatch, H_C),
        in_specs=[
            pl.BlockSpec((dseq, QK_PAD), lambda b, h: (off + b, h)),
            pl.BlockSpec((dseq, QK_PAD), lambda b, h: (off + b, h)),
            pl.BlockSpec((dseq, V_DIM), lambda b, h: (off + b, h)),
            pl.BlockSpec((past, QK_PAD), lambda b, h: (b, h)),
            pl.BlockSpec((past, V_DIM), lambda b, h: (b, h)),
        ],
        out_specs=pl.BlockSpec((dseq, V_DIM), lambda b, h: (b, h)),
        scratch_shapes=[pltpu.VMEM((past + dseq, QK_PAD), BF16),
                        pltpu.VMEM((V_DIM, past + dseq), BF16)],
        compiler_params=_params("arbitrary", "arbitrary"),
        name="mla_attn",
    )(q, k, v, kx, vx)


def _rope_tables(dseq, tm):
    t = np.arange(dseq)
    quarter = ROPE // 4
    freqs = jnp.asarray(ROPE_BASE, F32) ** (-jnp.arange(quarter, dtype=F32) / quarter)
    ang_r = jnp.asarray(t // GRID_W, F32)[:, None] * freqs
    ang_c = jnp.asarray(t % GRID_W, F32)[:, None] * freqs
    zeros = jnp.zeros((dseq, LANES - ROPE), F32)
    cos = jnp.concatenate([jnp.cos(ang_r), jnp.cos(ang_r), jnp.cos(ang_c), jnp.cos(ang_c), zeros], axis=1)
    sin = jnp.concatenate([-jnp.sin(ang_r), jnp.sin(ang_r), -jnp.sin(ang_c), jnp.sin(ang_c), zeros], axis=1)
    ident = jnp.concatenate([jnp.ones((tm, ROPE), F32), jnp.zeros((tm, LANES - ROPE), F32)], axis=1)
    return (jnp.concatenate([ident, cos], axis=0),
            jnp.concatenate([jnp.zeros((tm, LANES), F32), sin], axis=0))


def _pick_tile(pref, *lengths):
    t = pref
    while any(l % t for l in lengths):
        t //= 2
    return t


def kernel(x_prompt, x_sample, cache_nat_k, cache_nat_v, cache_mla_ckv, cache_mla_krope, c, c_ctx,
           norm1_g, norm2_g, w_ada, b_ada, w_in_ab, g_qn_a, g_kn_a, rpb_a, conv_b_w, w_out_ab,
           w_down_c, g_cq, g_ckv, w_uq_c, w_ukv_c, g_qn_c, g_kn_c, w_o_c, w_ffn_in, w_ffn_out):
    batch, seq, d = x_prompt.shape
    dbatch, dseq, _ = x_sample.shape
    depth = w_ada.shape[0]
    n_even = w_in_ab.shape[0]
    past = cache_nat_k.shape[3]
    nc, ns = batch * seq, dbatch * dseq
    n = nc + ns
    dff = w_ffn_out.shape[1]
    assert nc % dseq == 0 and dseq % seq == 0 and dseq % GRID_W == 0

    tm = _pick_tile(512, nc, dseq)
    tm_ab = _pick_tile(256, nc, dseq)
    tm_c = _pick_tile(512, nc, dseq)
    tm_f = _pick_tile(1024, nc, dseq)
    tf = _pick_tile(512, dff)
    assert tm % seq == 0

    x, x2, second = x_prompt.reshape(nc, d), x_sample.reshape(ns, d), "select"

    groups = 1 + dbatch
    gp = -(-groups // SUBLANES) * SUBLANES
    cond = jnp.concatenate([c_ctx[None, :], c, jnp.zeros((gp - groups, d), F32)], axis=0)
    mods_all = _adaln(cond, w_ada, b_ada).reshape(depth, gp, 6, d)

    rope_c, rope_s = _rope_tables(dseq, tm_c)
    ident_c = jnp.concatenate([jnp.ones((past, ROPE), F32), jnp.zeros((past, LANES - ROPE), F32)], axis=1)
    shape_kw = dict(nc=nc, dseq=dseq)

    w_in_ab_b = w_in_ab.astype(BF16)
    w_out_ab_b = w_out_ab.astype(BF16)
    w_o_c_b = w_o_c.astype(BF16)
    w_ffn_in_b = w_ffn_in.astype(BF16)
    w_ffn_out_b = w_ffn_out.astype(BF16)

    nat = None
    mla_ckv, mla_kr = [], []
    for l in range(depth):
        i = l // 2
        mods = mods_all[l]
        if l % 2 == 0:
            gq = (g_qn_a[i] * (DH_A ** -0.5 * LOG2E))[None, :]
            qkv, gb, z, nk, nv = _inproj_ab(
                x, x2, second, mods, norm1_g[l][None, :], w_in_ab_b, gq, g_kn_a[i][None, :], nat, i,
                seq=seq, batch=batch, n_even=n_even, tm=tm_ab, **shape_kw)
            nat = (nk, nv)
            ya_c = _ctx_attn(qkv, qkv, qkv, (0, 1, 2), nc=nc, batch=batch, seq=seq,
                             heads=H_A, dq=DH_A, dv=DH_A)
            ya_s = _nat_attn(rpb_a[i], qkv, cache_nat_k, cache_nat_v, i, dbatch=dbatch, **shape_kw)
            x, h2 = _outproj(x, x2, second, mods, norm2_g[l][None, :], ya_c, ya_s, w_out_ab_b, i,
                             (gb, z, conv_b_w[i]), seq=seq, tm=tm, **shape_kw)
        else:
            wd = jnp.pad(w_down_c[i], ((0, 0), (0, LANES - ROPE))).astype(BF16)
            wq = jnp.pad(w_uq_c[i].reshape(Q_LORA, H_C, QK_DIM),
                         ((0, 0), (0, 0), (0, QK_PAD - QK_DIM))).reshape(Q_LORA, H_C * QK_PAD).astype(BF16)
            wkv = w_ukv_c[i].astype(BF16)
            gq = jnp.pad(g_qn_c[i] * (QK_DIM ** -0.5 * LOG2E), (0, QK_PAD - QK_DIM))[None, :]
            gkn = g_kn_c[i][None, :NOPE]
            gkr = jnp.pad(g_kn_c[i][NOPE:], (0, LANES - ROPE))[None, :]
            q, k, v, ckv_n, kr = _inproj_c(
                x, x2, second, mods, norm1_g[l][None, :], wd, g_cq[i][None, :], g_ckv[i][None, :],
                wq, gq, wkv, gkn, gkr, rope_c, rope_s, tm=tm_c, **shape_kw)
            mla_ckv.append(ckv_n.reshape(batch, seq, KV_LORA))
            mla_kr.append(kr.reshape(batch, seq, ROPE))
            kx, vx = _cache_kv(
                cache_mla_ckv[:, i].reshape(dbatch * past, KV_LORA),
                jnp.pad(cache_mla_krope[:, i].reshape(dbatch * past, ROPE), ((0, 0), (0, LANES - ROPE))),
                wkv, gkn, gkr, ident_c, jnp.zeros((past, LANES), F32), tm=past)
            ya_c = _ctx_attn(q, k, v, (0, 0, 0), nc=nc, batch=batch, seq=seq,
                             heads=H_C, dq=QK_PAD, dv=V_DIM)
            ya_s = _mla_attn(q, k, v, kx, vx, dbatch=dbatch, past=past,
                             tq=_pick_tile(256, dseq), **shape_kw)
            x, h2 = _outproj(x, x2, second, mods, norm2_g[l][None, :], ya_c, ya_s, w_o_c_b, i,
                             seq=seq, tm=tm, **shape_kw)
        last = l == depth - 1
        x2, second = _ffn(x, h2, mods, w_ffn_in_b, w_ffn_out_b, l, last=last,
                          tm=tm if last else tm_f, tf=tf, **shape_kw), "add"

    return (x2[0].reshape(batch, seq, d), x2[1].reshape(dbatch, dseq, d), nat[0], nat[1],
            jnp.stack(mla_ckv, axis=1), jnp.stack(mla_kr, axis=1))
```

```python
import functools

import numpy as np
import jax
import jax.numpy as jnp
from jax import lax
from jax.experimental import pallas as pl
from jax.experimental.pallas import tpu as pltpu

F32 = jnp.float32
BF16 = jnp.bfloat16

EPS = 1e-6
GRID_W = 64
WIN_R = 8
WIN_C = 16
ROPE_BASE = 10000.0
H_A = 8
DH_A = 128
H_C = 16
Q_LORA = 512
KV_LORA = 256
NOPE = 128
ROPE = 64
V_DIM = 128
QK_DIM = NOPE + ROPE
QK_PAD = 256
LANES = 128
SUBLANES = 8
HALO = 16
NEG = -1e30
LOG2E = 1.4426950408889634

VMEM_LIMIT = 56 * 1024 * 1024
ROW_BLOCK = 128
ROW_BLOCK_AB = 256


def _params(*sem):
    return pltpu.CompilerParams(dimension_semantics=sem, vmem_limit_bytes=VMEM_LIMIT)


def _group(i, nct, per):
    return jnp.where(i < nct, 0, 1 + (i - nct) // per)


def _dot(a, b):
    return jnp.dot(a, b, preferred_element_type=F32)


def _dot_nt(a, b):
    return lax.dot_general(a, b, (((1,), (1,)), ((), ())), preferred_element_type=F32)


def _ada_kernel(c_ref, w_ref, b_ref, o_ref):
    c = c_ref[...]
    s = (c / (1.0 + jnp.exp(-c))).astype(BF16)
    o_ref[...] = _dot(s, w_ref[...].astype(BF16)) + b_ref[...]


def _adaln(cond, w_ada, b_ada):
    depth, d, n6 = w_ada.shape
    gp = cond.shape[0]
    tn = 1024 if n6 % 1024 == 0 else n6
    return pl.pallas_call(
        _ada_kernel,
        out_shape=jax.ShapeDtypeStruct((depth, gp, n6), F32),
        grid=(depth, n6 // tn),
        in_specs=[
            pl.BlockSpec((gp, d), lambda l, n: (0, 0)),
            pl.BlockSpec((None, d, tn), lambda l, n: (l, 0, n)),
            pl.BlockSpec((None, 1, tn), lambda l, n: (l, 0, n)),
        ],
        out_specs=pl.BlockSpec((None, gp, tn), lambda l, n: (l, 0, n)),
        compiler_params=_params("arbitrary", "arbitrary"),
        name="adaln",
    )(cond, w_ada, b_ada.reshape(depth, 1, n6))


def _row_spec(tm, d, nct, second, which):
    if second != "select":
        return pl.BlockSpec((tm, d), lambda i, *_: (i, 0))
    if which == 0:
        return pl.BlockSpec((tm, d), lambda i, *_: (jnp.minimum(i, nct - 1), 0))
    return pl.BlockSpec((tm, d), lambda i, *_: (jnp.maximum(i - nct, 0), 0))


def _residual(x_ref, x2_ref, rows, second, is_ctx):
    x = x_ref[rows, :]
    if second == "add":
        x = x + x2_ref[rows, :]
    elif second == "select":
        x = jnp.where(is_ctx, x, x2_ref[rows, :])
    return x


def _inproj_ab_kernel(x_ref, *rest, nct, seq, first, rb, second):
    x2_ref = None
    if second:
        x2_ref, *rest = rest
    mod_ref, g1_ref, w_ref, gq_ref, gk_ref, *rest = rest
    if first:
        qkv_ref, gb_ref, z_ref, natk_ref, natv_ref = rest
    else:
        _, _, qkv_ref, gb_ref, z_ref, natk_ref, natv_ref = rest
    i = pl.program_id(0)
    tm = x_ref.shape[0]
    na = H_A * DH_A
    is_ctx = i < nct
    gs = g1_ref[...] * (1.0 + mod_ref[1:2, :])

    def nat_store(nat_ref, rows, hh, val):
        bb, off = rows.start // seq, rows.start % seq
        if first:
            nat_ref[bb, 0, hh, off:off + rb, :] = val
        else:
            nat_ref[bb, hh, off:off + rb, :] = val

    def head_norm(rows, r, col, gain_ref, nat_ref):
        for hh in range(H_A):
            t = r[:, hh * DH_A:(hh + 1) * DH_A]
            ms = jnp.mean(t * t, axis=-1, keepdims=True)
            val = t * lax.rsqrt(ms + EPS) * gain_ref[...]
            qkv_ref[rows, col * na + hh * DH_A:col * na + (hh + 1) * DH_A] = val.astype(BF16)
            if nat_ref is not None:
                nat_store(nat_ref, rows, hh, val)

    def tile(ctx):
        if ctx and first and natk_ref.shape[1] > 1:
            natk_ref[:, 1:] = jnp.zeros_like(natk_ref[:, 1:])
            natv_ref[:, 1:] = jnp.zeros_like(natv_ref[:, 1:])
        for r0 in range(0, tm, rb):
            rows = slice(r0, r0 + rb)
            x = _residual(x_ref, x2_ref, rows, second, is_ctx)
            ms = jnp.mean(x * x, axis=-1, keepdims=True)
            h = (x * lax.rsqrt(ms + EPS) * gs + mod_ref[0:1, :]).astype(BF16)

            def proj(col):
                return _dot(h, w_ref[:, col * na:(col + 1) * na])

            head_norm(rows, proj(0), 0, gq_ref, None)
            head_norm(rows, proj(1), 1, gk_ref, natk_ref if ctx else None)
            v = proj(2)
            qkv_ref[rows, 2 * na:3 * na] = v.astype(BF16)
            if ctx:
                for hh in range(H_A):
                    nat_store(natv_ref, rows, hh, v[:, hh * DH_A:(hh + 1) * DH_A])
            gb_ref[rows, :] = proj(3).astype(BF16)
            gc = proj(4)
            z_ref[rows, :] = (gc * proj(5)).astype(BF16)

    pl.when(is_ctx)(lambda: tile(True))
    pl.when(jnp.logical_not(is_ctx))(lambda: tile(False))


def _inproj_ab(x, x2, second, mods, g1, w_in, gq, gk, nat_prev, slot,
               *, nc, seq, dseq, batch, n_even, tm):
    d = x.shape[1]
    n = nc + (x2.shape[0] if second == "select" else x.shape[0] - nc)
    na = H_A * DH_A
    nct = nc // tm
    per = dseq // tm
    first = nat_prev is None
    kern = functools.partial(_inproj_ab_kernel, nct=nct, seq=seq, first=first,
                             rb=_pick_tile(ROW_BLOCK_AB, tm, seq), second=second)
    nat_shape = jax.ShapeDtypeStruct((batch, n_even, H_A, seq, DH_A), F32)
    if first:
        nat_spec = pl.BlockSpec((tm // seq, n_even, H_A, seq, DH_A),
                                lambda i: (jnp.minimum(i, nct - 1), 0, 0, 0, 0))
    else:
        nat_spec = pl.BlockSpec((tm // seq, None, H_A, seq, DH_A),
                                lambda i: (jnp.minimum(i, nct - 1), slot, 0, 0, 0))
    in_specs = [_row_spec(tm, d, nct, second, 0)] + ([_row_spec(tm, d, nct, second, 1)] if second else []) + [
        pl.BlockSpec((None, 6, d), lambda i: (_group(i, nct, per), 0, 0)),
        pl.BlockSpec((1, d), lambda i: (0, 0)),
        pl.BlockSpec((None,) + w_in.shape[1:], lambda i: (slot, 0, 0), pipeline_mode=pl.Buffered(1)),
        pl.BlockSpec((1, DH_A), lambda i: (0, 0)),
        pl.BlockSpec((1, DH_A), lambda i: (0, 0)),
    ]
    args = [x] + ([x2] if second else []) + [mods, g1, w_in, gq, gk]
    aliases = {}
    if not first:
        in_specs += [pl.BlockSpec(memory_space=pl.ANY)] * 2
        aliases = {len(args): 3, len(args) + 1: 4}
        args += list(nat_prev)
    return pl.pallas_call(
        kern,
        out_shape=(jax.ShapeDtypeStruct((n, 3 * na), BF16),
                   jax.ShapeDtypeStruct((n, na), BF16),
                   jax.ShapeDtypeStruct((n, na), BF16),
                   nat_shape, nat_shape),
        grid=(n // tm,),
        in_specs=in_specs,
        out_specs=(
            pl.BlockSpec((tm, 3 * na), lambda i: (i, 0)),
            pl.BlockSpec((tm, na), lambda i: (i, 0)),
            pl.BlockSpec((tm, na), lambda i: (i, 0)),
            nat_spec, nat_spec,
        ),
        input_output_aliases=aliases,
        compiler_params=_params("arbitrary"),
        name="inproj_ab",
    )(*args)


def _ctx_attn_kernel(q_ref, k_ref, v_ref, o_ref, *, heads, dq, dv):
    def scores(hh):
        return _dot_nt(q_ref[:, hh * dq:(hh + 1) * dq], k_ref[:, hh * dq:(hh + 1) * dq])

    s_next = scores(0)
    for hh in range(heads):
        s = s_next
        if hh + 1 < heads:
            s_next = scores(hh + 1)
        m = jnp.max(s, axis=-1, keepdims=True)
        p = jnp.exp2(s - m)
        l = jnp.sum(p, axis=-1, keepdims=True)
        o = _dot(p.astype(BF16), v_ref[:, hh * dv:(hh + 1) * dv]) / l
        o_ref[:, hh * dv:(hh + 1) * dv] = o.astype(BF16)


def _ctx_attn(q_arr, k_arr, v_arr, cols, *, nc, batch, seq, heads, dq, dv):
    cq, ck, cv = cols
    kern = functools.partial(_ctx_attn_kernel, heads=heads, dq=dq, dv=dv)
    return pl.pallas_call(
        kern,
        out_shape=jax.ShapeDtypeStruct((nc, heads * dv), BF16),
        grid=(batch,),
        in_specs=[
            pl.BlockSpec((seq, heads * dq), lambda b: (b, cq)),
            pl.BlockSpec((seq, heads * dq), lambda b: (b, ck)),
            pl.BlockSpec((seq, heads * dv), lambda b: (b, cv)),
        ],
        out_specs=pl.BlockSpec((seq, heads * dv), lambda b: (b, 0)),
        compiler_params=_params("arbitrary"),
        name="ctx_attn",
    )(q_arr, k_arr, v_arr)


NAT_QR = 8
NAT_KR = 16


def _nat_window(a, rows):
    return int(np.clip(a * NAT_QR - WIN_R // 2, 0, rows - NAT_KR))


def _nat_attn_kernel(rpb_ref, q_ref, k_ref, v_ref, ck_ref, cv_ref, o_ref,
                     t2_ref, bias_ref, *, rows):
    hh = pl.program_id(0)
    b = pl.program_id(1)
    nblk = rows // NAT_QR
    n_dr = 2 * WIN_R - 1
    n_dc = 2 * WIN_C - 1

    @pl.when(b == 0)
    def _():
        j = lax.broadcasted_iota(jnp.int32, (GRID_W, LANES), 0)
        kc = lax.broadcasted_iota(jnp.int32, (GRID_W, LANES), 1) % GRID_W
        cs = jnp.clip(j - WIN_C // 2, 0, GRID_W - WIN_C)
        col_in = (kc >= cs) & (kc < cs + WIN_C)
        dc = kc - j + (WIN_C - 1)
        for d in range(n_dr):
            t = jnp.full((GRID_W, LANES), NEG, F32)
            for c in range(n_dc):
                t = jnp.where(col_in & (dc == c), rpb_ref[hh, d * n_dc + c] * LOG2E, t)
            t2_ref[d] = t
        lane = lax.broadcasted_iota(jnp.int32, (GRID_W, LANES), 1)
        negt = jnp.full((GRID_W, LANES), NEG, F32)
        for a in range(nblk):
            ws = _nat_window(a, rows)
            for qr in range(NAT_QR):
                r = a * NAT_QR + qr
                rs = int(np.clip(r - WIN_R // 2, 0, rows - WIN_R))
                for pp in range(NAT_KR // 2):
                    halves = []
                    for kk in (2 * pp, 2 * pp + 1):
                        kr = ws + kk
                        halves.append(t2_ref[kr - r + WIN_R - 1] if rs <= kr < rs + WIN_R else None)
                    if halves[0] is None and halves[1] is None:
                        blk = negt
                    else:
                        left = negt if halves[0] is None else halves[0]
                        right = negt if halves[1] is None else halves[1]
                        blk = jnp.where(lane < GRID_W, left, right)
                    bias_ref[a, qr * GRID_W:(qr + 1) * GRID_W, pp * LANES:(pp + 1) * LANES] = blk

    ck = ck_ref[...].astype(BF16)
    cv = cv_ref[...].astype(BF16)
    tq = NAT_QR * GRID_W
    tk = NAT_KR * GRID_W

    def scores(a):
        ws = _nat_window(a, rows)
        q = q_ref[a * tq:(a + 1) * tq, :]
        kw = k_ref[ws * GRID_W:ws * GRID_W + tk, :]
        return _dot_nt(q, kw) + bias_ref[a], _dot_nt(q, ck)

    s_next = scores(0)
    for a in range(nblk):
        s_loc, s_ctx = s_next
        if a + 1 < nblk:
            s_next = scores(a + 1)
        ws = _nat_window(a, rows)
        vw = v_ref[ws * GRID_W:ws * GRID_W + tk, :]
        m = jnp.maximum(jnp.max(s_loc, axis=-1, keepdims=True),
                        jnp.max(s_ctx, axis=-1, keepdims=True))
        p_loc = jnp.exp2(s_loc - m)
        p_ctx = jnp.exp2(s_ctx - m)
        l = jnp.sum(p_loc, axis=-1, keepdims=True) + jnp.sum(p_ctx, axis=-1, keepdims=True)
        o = (_dot(p_loc.astype(BF16), vw) + _dot(p_ctx.astype(BF16), cv)) / l
        o_ref[a * tq:(a + 1) * tq, :] = o.astype(BF16)


def _nat_attn(rpb, qkv, cache_k, cache_v, slot, *, nc, dseq, dbatch):
    rows = dseq // GRID_W
    assert rows % NAT_QR == 0 and rows >= NAT_KR
    nblk = rows // NAT_QR
    past = cache_k.shape[3]
    off = nc // dseq
    n_dr, n_dc = 2 * WIN_R - 1, 2 * WIN_C - 1
    kern = functools.partial(_nat_attn_kernel, rows=rows)
    cache_spec = pl.BlockSpec((None, None, None, past, DH_A), lambda h, b: (b, slot, h, 0, 0))
    return pl.pallas_call(
        kern,
        out_shape=jax.ShapeDtypeStruct((dbatch * dseq, H_A * DH_A), BF16),
        grid=(H_A, dbatch),
        in_specs=[
            pl.BlockSpec(memory_space=pltpu.SMEM),
            pl.BlockSpec((dseq, DH_A), lambda h, b: (off + b, h)),
            pl.BlockSpec((dseq, DH_A), lambda h, b: (off + b, H_A + h)),
            pl.BlockSpec((dseq, DH_A), lambda h, b: (off + b, 2 * H_A + h)),
            cache_spec, cache_spec,
        ],
        out_specs=pl.BlockSpec((dseq, DH_A), lambda h, b: (b, h)),
        scratch_shapes=[pltpu.VMEM((n_dr, GRID_W, LANES), F32),
                        pltpu.VMEM((nblk, NAT_QR * GRID_W, NAT_KR * GRID_W), F32)],
        compiler_params=_params("arbitrary", "arbitrary"),
        name="nat_attn",
    )(rpb.reshape(H_A, n_dr * n_dc), qkv, qkv, qkv, cache_k, cache_v)


def _outproj_kernel(x_ref, *refs, conv, nct, seq, dseq, rb, second):
    x2_ref = None
    if second:
        x2_ref, *refs = refs
    if conv:
        (mod_ref, g2_ref, yac_ref, yas_ref, gb_ref, z_ref, zp_ref, zn_ref, cw_ref, w_ref,
         o_ref, h2_ref, zs_ref) = refs
    else:
        mod_ref, g2_ref, yac_ref, yas_ref, w_ref, o_ref, h2_ref = refs
    i = pl.program_id(0)
    tm = x_ref.shape[0]
    ka = yac_ref.shape[1]
    if conv:
        zs_ref[0:HALO, :] = zp_ref[...].astype(F32)
        zs_ref[HALO:HALO + tm, :] = z_ref[...].astype(F32)
        zs_ref[HALO + tm:2 * HALO + tm, :] = zn_ref[...].astype(F32)
        last = jnp.where(i < nct, seq - 1, dseq - 1)
    gate = mod_ref[2:3, :]
    gs = g2_ref[...] * (1.0 + mod_ref[4:5, :])
    shift = mod_ref[3:4, :]
    for r0 in range(0, tm, rb):
        rows = slice(r0, r0 + rb)
        ya = jnp.where(i < nct, yac_ref[rows, :], yas_ref[rows, :])
        y = _dot(ya, w_ref[0:ka, :])
        if conv:
            row = r0 + lax.broadcasted_iota(jnp.int32, (rb, 1), 0)
            pos = jnp.where(i < nct, row % seq, ((i - nct) * tm + row) % dseq)
            z_prev = jnp.where(pos == 0, 0.0, zs_ref[HALO - 1 + r0:HALO - 1 + r0 + rb, :])
            z_next = jnp.where(pos == last, 0.0, zs_ref[HALO + 1 + r0:HALO + 1 + r0 + rb, :])
            z_mid = zs_ref[HALO + r0:HALO + r0 + rb, :]
            cv = cw_ref[0:1, :] * z_prev + cw_ref[1:2, :] * z_mid + cw_ref[2:3, :] * z_next
            yb = gb_ref[rows, :].astype(F32) * cv
            y = y + _dot(yb.astype(BF16), w_ref[ka:, :])
        xn = _residual(x_ref, x2_ref, rows, second, i < nct) + gate * y
        o_ref[rows, :] = xn
        ms = jnp.mean(xn * xn, axis=-1, keepdims=True)
        h2_ref[rows, :] = (xn * lax.rsqrt(ms + EPS) * gs + shift).astype(BF16)


def _outproj(x, x2, second, mods, g2, ya_c, ya_s, w_out, slot, conv_in=None, *, nc, seq, dseq, tm):
    d = x.shape[1]
    n = nc + ya_s.shape[0]
    ka = ya_c.shape[1]
    nct = nc // tm
    per = dseq // tm
    conv = conv_in is not None
    kern = functools.partial(_outproj_kernel, conv=conv, nct=nct, seq=seq, dseq=dseq,
                             rb=_pick_tile(ROW_BLOCK, tm), second=second)
    in_specs = [_row_spec(tm, d, nct, second, 0)] + ([_row_spec(tm, d, nct, second, 1)] if second else []) + [
        pl.BlockSpec((None, 6, d), lambda i: (_group(i, nct, per), 0, 0)),
        pl.BlockSpec((1, d), lambda i: (0, 0)),
        pl.BlockSpec((tm, ka), lambda i: (jnp.minimum(i, nct - 1), 0)),
        pl.BlockSpec((tm, ka), lambda i: (jnp.maximum(i - nct, 0), 0)),
    ]
    args = [x] + ([x2] if second else []) + [mods, g2, ya_c, ya_s]
    scratch = []
    if conv:
        gb, z, cw = conv_in
        nb = z.shape[1]
        th = tm // HALO
        nh = n // HALO
        in_specs += [
            pl.BlockSpec((tm, nb), lambda i: (i, 0)),
            pl.BlockSpec((tm, nb), lambda i: (i, 0)),
            pl.BlockSpec((HALO, nb), lambda i: (jnp.maximum(i * th - 1, 0), 0)),
            pl.BlockSpec((HALO, nb), lambda i: (jnp.minimum((i + 1) * th, nh - 1), 0)),
            pl.BlockSpec((3, nb), lambda i: (0, 0)),
        ]
        args += [gb, z, z, z, cw]
        scratch = [pltpu.VMEM((tm + 2 * HALO, nb), F32)]
    in_specs.append(pl.BlockSpec((None,) + w_out.shape[1:], lambda i: (slot, 0, 0),
                                 pipeline_mode=pl.Buffered(1)))
    args.append(w_out)
    return pl.pallas_call(
        kern,
        out_shape=(jax.ShapeDtypeStruct((n, d), F32), jax.ShapeDtypeStruct((n, d), BF16)),
        grid=(n // tm,),
        in_specs=in_specs,
        out_specs=(pl.BlockSpec((tm, d), lambda i: (i, 0)), pl.BlockSpec((tm, d), lambda i: (i, 0))),
        scratch_shapes=scratch,
        compiler_params=_params("arbitrary"),
        name="outproj_conv" if conv else "outproj",
    )(*args)


def _ffn_kernel(*refs, last, nct, cast_next):
    if last:
        x_ref, h_ref, mod_ref, wg_ref, wu_ref, wo_ref, oc_ref, os_ref, acc_ref = refs
    elif cast_next:
        (h_ref, mod_ref, wg_ref, wu_ref, wo_ref, nin_ref, nout_ref,
         acc_ref, nin_b_ref, nout_b_ref) = refs
    else:
        h_ref, mod_ref, wg_ref, wu_ref, wo_ref, acc_ref = refs
    i = pl.program_id(0)
    f = pl.program_id(1)

    def step(start):
        h = h_ref[...]
        g = _dot(h, wg_ref[...])
        u = _dot(h, wu_ref[...])
        a = (g / (1.0 + jnp.exp(-g))) * u
        part = _dot(a.astype(BF16), wo_ref[...])
        acc_ref[...] = part if start else acc_ref[...] + part
        if cast_next:
            nin_b_ref[...] = nin_ref[...].astype(BF16)
            nout_b_ref[...] = nout_ref[...].astype(BF16)

    pl.when(f == 0)(lambda: step(True))
    pl.when(f != 0)(lambda: step(False))

    @pl.when(f == pl.num_programs(1) - 1)
    def _():
        delta = mod_ref[5:6, :] * acc_ref[...]
        if last:
            @pl.when(i < nct)
            def _():
                oc_ref[...] = x_ref[...] + delta

            @pl.when(i >= nct)
            def _():
                os_ref[...] = x_ref[...] + delta
        else:
            acc_ref[...] = delta


def _slab_rows(total, steps):
    r = HALO
    while total % r or total // r > steps:
        r += HALO
    return r


def _ffn(x, h2, mods, w_in, w_out, next_w, layer, *, last, nc, dseq, tm, tf):
    n, d = h2.shape
    dff = w_out.shape[0]
    nf = dff // tf
    nct = nc // tm
    per = dseq // tm
    cast_next = next_w is not None
    kern = functools.partial(_ffn_kernel, last=last, nct=nct, cast_next=cast_next)
    row_spec = pl.BlockSpec((tm, d), lambda i, f: (i, 0))
    in_specs = [
        row_spec,
        pl.BlockSpec((None, 6, d), lambda i, f: (_group(i, nct, per), 0, 0)),
        pl.BlockSpec((d, tf), lambda i, f: (0, f)),
        pl.BlockSpec((d, tf), lambda i, f: (0, nf + f)),
        pl.BlockSpec((tf, d), lambda i, f: (f, 0)),
    ]
    args = [h2, mods, w_in, w_in, w_out]
    scratch = []
    if last:
        in_specs.insert(0, row_spec)
        args.insert(0, x)
        out_shape = (jax.ShapeDtypeStruct((nc, d), F32), jax.ShapeDtypeStruct((n - nc, d), F32))
        out_specs = (pl.BlockSpec((tm, d), lambda i, f: (jnp.minimum(i, nct - 1), 0)),
                     pl.BlockSpec((tm, d), lambda i, f: (jnp.maximum(i - nct, 0), 0)))
        scratch = [pltpu.VMEM((tm, d), F32)]
    else:
        out_shape = jax.ShapeDtypeStruct((n, d), F32)
        out_specs = row_spec
    if cast_next:
        steps = (n // tm) * nf
        r_in, r_out = _slab_rows(d, steps), _slab_rows(dff, steps)
        slab = lambda nslab: (lambda i, f: (jnp.minimum(i * nf + f, nslab - 1), 0))
        slab3 = lambda nslab: (lambda i, f: (layer + 1, jnp.minimum(i * nf + f, nslab - 1), 0))
        in_specs += [pl.BlockSpec((None, r_in, 2 * dff), slab3(d // r_in)),
                     pl.BlockSpec((None, r_out, d), slab3(dff // r_out))]
        args += list(next_w)
        out_shape = (out_shape, jax.ShapeDtypeStruct((d, 2 * dff), BF16),
                     jax.ShapeDtypeStruct((dff, d), BF16))
        out_specs = (out_specs, pl.BlockSpec((r_in, 2 * dff), slab(d // r_in)),
                     pl.BlockSpec((r_out, d), slab(dff // r_out)))
    return pl.pallas_call(
        kern,
        out_shape=out_shape,
        grid=(n // tm, nf),
        in_specs=in_specs,
        out_specs=out_specs,
        scratch_shapes=scratch,
        compiler_params=_params("arbitrary", "arbitrary"),
        name="ffn",
    )(*args)


def _rope(x, c, s):
    lane = lax.broadcasted_iota(jnp.int32, x.shape, 1)
    quarter = ROPE // 4
    swapped = jnp.where(lane % (2 * quarter) < quarter,
                        pltpu.roll(x, LANES - quarter, 1), pltpu.roll(x, quarter, 1))
    return x * c + swapped * s


def _mla_kv_expand(ckv_bf16, krp, wkv_ref, gkn_ref, gkr_ref, c, s, k_ref, v_ref, rows=slice(None)):
    kv = _dot(ckv_bf16, wkv_ref[...])
    krr = _rope(krp * gkr_ref[...], c, s)
    ss_kr = jnp.sum(krp * krp, axis=-1, keepdims=True)
    for hh in range(H_C):
        base = hh * (NOPE + V_DIM)
        kn = kv[:, base:base + NOPE]
        rstd = lax.rsqrt((jnp.sum(kn * kn, axis=-1, keepdims=True) + ss_kr) / QK_DIM + EPS)
        k_ref[rows, hh * QK_PAD:hh * QK_PAD + NOPE] = (kn * rstd * gkn_ref[...]).astype(BF16)
        k_ref[rows, hh * QK_PAD + NOPE:(hh + 1) * QK_PAD] = (krr * rstd).astype(BF16)
        v_ref[rows, hh * V_DIM:(hh + 1) * V_DIM] = kv[:, base + NOPE:base + NOPE + V_DIM].astype(BF16)


def _inproj_c_kernel(x_ref, *rest, nct, rb, second):
    x2_ref = None
    if second:
        x2_ref, *rest = rest
    (mod_ref, g1_ref, wd_ref, gcq_ref, gckv_ref, wq_ref, gq_ref, wkv_ref, gkn_ref, gkr_ref,
     c_ref, s_ref, q_ref, k_ref, v_ref, ckv_ref, kr_ref) = rest
    i = pl.program_id(0)
    tm = x_ref.shape[0]
    blocks = [slice(r * rb, (r + 1) * rb) for r in range(tm // rb)]
    gs = g1_ref[...] * (1.0 + mod_ref[1:2, :])

    def latents(rows):
        x = _residual(x_ref, x2_ref, rows, second, i < nct)
        ms = jnp.mean(x * x, axis=-1, keepdims=True)
        h = (x * lax.rsqrt(ms + EPS) * gs + mod_ref[0:1, :]).astype(BF16)
        dn = _dot(h, wd_ref[...])
        cq = dn[:, :Q_LORA]
        ckv = dn[:, Q_LORA:Q_LORA + KV_LORA]
        krp = dn[:, Q_LORA + KV_LORA:]
        cqn = cq * lax.rsqrt(jnp.mean(cq * cq, axis=-1, keepdims=True) + EPS) * gcq_ref[...]
        ckvn = ckv * lax.rsqrt(jnp.mean(ckv * ckv, axis=-1, keepdims=True) + EPS) * gckv_ref[...]
        return cqn.astype(BF16), ckvn, krp

    def expand(rows, cqn, ckvn, krp, ctx):
        if ctx:
            ckv_ref[rows, :] = ckvn
            kr_ref[rows, :] = krp[:, :ROPE]
        c = c_ref[rows, :]
        s = s_ref[rows, :]
        q = _dot(cqn, wq_ref[...])
        for hh in range(H_C):
            qh = q[:, hh * QK_PAD:(hh + 1) * QK_PAD]
            rstd = lax.rsqrt(jnp.sum(qh * qh, axis=-1, keepdims=True) / QK_DIM + EPS)
            qn = qh * rstd * gq_ref[...]
            q_ref[rows, hh * QK_PAD:hh * QK_PAD + NOPE] = qn[:, :NOPE].astype(BF16)
            q_ref[rows, hh * QK_PAD + NOPE:(hh + 1) * QK_PAD] = _rope(qn[:, NOPE:], c, s).astype(BF16)
        _mla_kv_expand(ckvn.astype(BF16), krp, wkv_ref, gkn_ref, gkr_ref, c, s, k_ref, v_ref, rows)

    def tile(ctx):
        nxt = latents(blocks[0])
        for r, rows in enumerate(blocks):
            cur = nxt
            if r + 1 < len(blocks):
                nxt = latents(blocks[r + 1])
            expand(rows, *cur, ctx)

    pl.when(i < nct)(lambda: tile(True))
    pl.when(i >= nct)(lambda: tile(False))


def _inproj_c(x, x2, second, mods, g1, wd, gcq, gckv, wq, gq, wkv, gkn, gkr, rope_c, rope_s,
              *, nc, dseq, tm):
    d = x.shape[1]
    n = nc + (x2.shape[0] if second == "select" else x.shape[0] - nc)
    nct = nc // tm
    per = dseq // tm
    kern = functools.partial(_inproj_c_kernel, nct=nct, rb=_pick_tile(ROW_BLOCK, tm), second=second)
    const = lambda shape: pl.BlockSpec(shape, lambda i: (0,) * len(shape),
                                       pipeline_mode=pl.Buffered(1))
    rope_idx = lambda i: (jnp.where(i < nct, 0, 1 + (i - nct) % per), 0)
    ctx_idx = lambda i: (jnp.minimum(i, nct - 1), 0)
    return pl.pallas_call(
        kern,
        out_shape=(jax.ShapeDtypeStruct((n, H_C * QK_PAD), BF16),
                   jax.ShapeDtypeStruct((n, H_C * QK_PAD), BF16),
                   jax.ShapeDtypeStruct((n, H_C * V_DIM), BF16),
                   jax.ShapeDtypeStruct((nc, KV_LORA), F32),
                   jax.ShapeDtypeStruct((nc, ROPE), F32)),
        grid=(n // tm,),
        in_specs=[_row_spec(tm, d, nct, second, 0)] + ([_row_spec(tm, d, nct, second, 1)] if second else []) + [
            pl.BlockSpec((None, 6, d), lambda i: (_group(i, nct, per), 0, 0)),
            const((1, d)), const(wd.shape), const((1, Q_LORA)), const((1, KV_LORA)),
            const(wq.shape), const((1, QK_PAD)), const(wkv.shape),
            const((1, NOPE)), const((1, LANES)),
            pl.BlockSpec((tm, LANES), rope_idx),
            pl.BlockSpec((tm, LANES), rope_idx),
        ],
        out_specs=(
            pl.BlockSpec((tm, H_C * QK_PAD), lambda i: (i, 0)),
            pl.BlockSpec((tm, H_C * QK_PAD), lambda i: (i, 0)),
            pl.BlockSpec((tm, H_C * V_DIM), lambda i: (i, 0)),
            pl.BlockSpec((tm, KV_LORA), ctx_idx),
            pl.BlockSpec((tm, ROPE), ctx_idx),
        ),
        compiler_params=_params("arbitrary"),
        name="inproj_c",
    )(*([x] + ([x2] if second else []) + [mods, g1, wd, gcq, gckv, wq, gq, wkv, gkn, gkr, rope_c, rope_s]))


def _cache_kv_kernel(ckv_ref, krp_ref, wkv_ref, gkn_ref, gkr_ref, c_ref, s_ref, k_ref, v_ref):
    _mla_kv_expand(ckv_ref[...].astype(BF16), krp_ref[...], wkv_ref, gkn_ref, gkr_ref,
                   c_ref[...], s_ref[...], k_ref, v_ref)


def _cache_kv(ckv, krp, wkv, gkn, gkr, rope_c, rope_s, *, tm):
    n = ckv.shape[0]
    const = lambda shape: pl.BlockSpec(shape, lambda i: (0,) * len(shape))
    return pl.pallas_call(
        _cache_kv_kernel,
        out_shape=(jax.ShapeDtypeStruct((n, H_C * QK_PAD), BF16),
                   jax.ShapeDtypeStruct((n, H_C * V_DIM), BF16)),
        grid=(n // tm,),
        in_specs=[
            pl.BlockSpec((tm, KV_LORA), lambda i: (i, 0)),
            pl.BlockSpec((tm, LANES), lambda i: (i, 0)),
            const(wkv.shape), const((1, NOPE)), const((1, LANES)),
            const((tm, LANES)), const((tm, LANES)),
        ],
        out_specs=(pl.BlockSpec((tm, H_C * QK_PAD), lambda i: (i, 0)),
                   pl.BlockSpec((tm, H_C * V_DIM), lambda i: (i, 0))),
        compiler_params=_params("arbitrary"),
        name="cache_kv",
    )(ckv, krp, wkv, gkn, gkr, rope_c, rope_s)


def _mla_attn_kernel(q_ref, k_ref, v_ref, kx_ref, vx_ref, o_ref, kall_ref, vt_ref, *, tq):
    past = kx_ref.shape[0]
    kall_ref[0:past, :] = kx_ref[...]
    kall_ref[past:, :] = k_ref[...]
    vt_ref[:, 0:past] = vx_ref[...].astype(F32).T.astype(BF16)
    vt_ref[:, past:] = v_ref[...].astype(F32).T.astype(BF16)

    def scores(t):
        q = q_ref[t * tq:(t + 1) * tq, :]
        return _dot_nt(kall_ref[...], q)

    nblk = q_ref.shape[0] // tq
    s_next = scores(0)
    for t in range(nblk):
        s = s_next
        if t + 1 < nblk:
            s_next = scores(t + 1)
        m = jnp.max(s, axis=0, keepdims=True)
        p = jnp.exp2(s - m)
        l = jnp.sum(p, axis=0, keepdims=True)
        ot = _dot(vt_ref[...], p.astype(BF16)) / l
        o_ref[t * tq:(t + 1) * tq, :] = ot.T.astype(BF16)


def _mla_attn(q, k, v, kx, vx, *, nc, dseq, dbatch, past, tq):
    off = nc // dseq
    kern = functools.partial(_mla_attn_kernel, tq=tq)
    return pl.pallas_call(
        kern,
        out_shape=jax.ShapeDtypeStruct((dbatch * dseq, H_C * V_DIM), BF16),
        grid=(dbatch, H_C),
        in_specs=[
            pl.BlockSpec((dseq, QK_PAD), lambda b, h: (off + b, h)),
            pl.BlockSpec((dseq, QK_PAD), lambda b, h: (off + b, h)),
            pl.BlockSpec((dseq, V_DIM), lambda b, h: (off + b, h)),
            pl.BlockSpec((past, QK_PAD), lambda b, h: (b, h)),
            pl.BlockSpec((past, V_DIM), lambda b, h: (b, h)),
        ],
        out_specs=pl.BlockSpec((dseq, V_DIM), lambda b, h: (b, h)),
        scratch_shapes=[pltpu.VMEM((past + dseq, QK_PAD), BF16),
                        pltpu.VMEM((V_DIM, past + dseq), BF16)],
        compiler_params=_params("arbitrary", "arbitrary"),
        name="mla_attn",
    )(q, k, v, kx, vx)


def _rope_tables(dseq, tm):
    t = np.arange(dseq)
    quarter = ROPE // 4
    freqs = jnp.asarray(ROPE_BASE, F32) ** (-jnp.arange(quarter, dtype=F32) / quarter)
    ang_r = jnp.asarray(t // GRID_W, F32)[:, None] * freqs
    ang_c = jnp.asarray(t % GRID_W, F32)[:, None] * freqs
    zeros = jnp.zeros((dseq, LANES - ROPE), F32)
    cos = jnp.concatenate([jnp.cos(ang_r), jnp.cos(ang_r), jnp.cos(ang_c), jnp.cos(ang_c), zeros], axis=1)
    sin = jnp.concatenate([-jnp.sin(ang_r), jnp.sin(ang_r), -jnp.sin(ang_c), jnp.sin(ang_c), zeros], axis=1)
    ident = jnp.concatenate([jnp.ones((tm, ROPE), F32), jnp.zeros((tm, LANES - ROPE), F32)], axis=1)
    return (jnp.concatenate([ident, cos], axis=0),
            jnp.concatenate([jnp.zeros((tm, LANES), F32), sin], axis=0))


def _pick_tile(pref, *lengths):
    t = pref
    while any(l % t for l in lengths):
        t //= 2
    return t


def kernel(x_prompt, x_sample, cache_nat_k, cache_nat_v, cache_mla_ckv, cache_mla_krope, c, c_ctx,
           norm1_g, norm2_g, w_ada, b_ada, w_in_ab, g_qn_a, g_kn_a, rpb_a, conv_b_w, w_out_ab,
           w_down_c, g_cq, g_ckv, w_uq_c, w_ukv_c, g_qn_c, g_kn_c, w_o_c, w_ffn_in, w_ffn_out):
    batch, seq, d = x_prompt.shape
    dbatch, dseq, _ = x_sample.shape
    depth = w_ada.shape[0]
    n_even = w_in_ab.shape[0]
    past = cache_nat_k.shape[3]
    nc, ns = batch * seq, dbatch * dseq
    n = nc + ns
    dff = w_ffn_out.shape[1]
    assert nc % dseq == 0 and dseq % seq == 0 and dseq % GRID_W == 0

    tm = _pick_tile(512, nc, dseq)
    tm_ab = _pick_tile(256, nc, dseq)
    tm_c = _pick_tile(512, nc, dseq)
    tm_f = _pick_tile(1024, nc, dseq)
    tf = _pick_tile(512, dff)
    assert tm % seq == 0

    x, x2, second = x_prompt.reshape(nc, d), x_sample.reshape(ns, d), "select"

    groups = 1 + dbatch
    gp = -(-groups // SUBLANES) * SUBLANES
    cond = jnp.concatenate([c_ctx[None, :], c, jnp.zeros((gp - groups, d), F32)], axis=0)
    mods_all = _adaln(cond, w_ada, b_ada).reshape(depth, gp, 6, d)

    rope_c, rope_s = _rope_tables(dseq, tm_c)
    ident_c = jnp.concatenate([jnp.ones((past, ROPE), F32), jnp.zeros((past, LANES - ROPE), F32)], axis=1)
    shape_kw = dict(nc=nc, dseq=dseq)

    w_in_ab_b = w_in_ab.astype(BF16)
    w_out_ab_b = w_out_ab.astype(BF16)
    w_o_c_b = w_o_c.astype(BF16)
    w_ffn_b = (w_ffn_in[0].astype(BF16), w_ffn_out[0].astype(BF16))

    nat = None
    mla_ckv, mla_kr = [], []
    for l in range(depth):
        i = l // 2
        mods = mods_all[l]
        if l % 2 == 0:
            gq = (g_qn_a[i] * (DH_A ** -0.5 * LOG2E))[None, :]
            qkv, gb, z, nk, nv = _inproj_ab(
                x, x2, second, mods, norm1_g[l][None, :], w_in_ab_b, gq, g_kn_a[i][None, :], nat, i,
                seq=seq, batch=batch, n_even=n_even, tm=tm_ab, **shape_kw)
            nat = (nk, nv)
            ya_c = _ctx_attn(qkv, qkv, qkv, (0, 1, 2), nc=nc, batch=batch, seq=seq,
                             heads=H_A, dq=DH_A, dv=DH_A)
            ya_s = _nat_attn(rpb_a[i], qkv, cache_nat_k, cache_nat_v, i, dbatch=dbatch, **shape_kw)
            x, h2 = _outproj(x, x2, second, mods, norm2_g[l][None, :], ya_c, ya_s, w_out_ab_b, i,
                             (gb, z, conv_b_w[i]), seq=seq, tm=tm, **shape_kw)
        else:
            wd = jnp.pad(w_down_c[i], ((0, 0), (0, LANES - ROPE))).astype(BF16)
            wq = jnp.pad(w_uq_c[i].reshape(Q_LORA, H_C, QK_DIM),
                         ((0, 0), (0, 0), (0, QK_PAD - QK_DIM))).reshape(Q_LORA, H_C * QK_PAD).astype(BF16)
            wkv = w_ukv_c[i].astype(BF16)
            gq = jnp.pad(g_qn_c[i] * (QK_DIM ** -0.5 * LOG2E), (0, QK_PAD - QK_DIM))[None, :]
            gkn = g_kn_c[i][None, :NOPE]
            gkr = jnp.pad(g_kn_c[i][NOPE:], (0, LANES - ROPE))[None, :]
            q, k, v, ckv_n, kr = _inproj_c(
                x, x2, second, mods, norm1_g[l][None, :], wd, g_cq[i][None, :], g_ckv[i][None, :],
                wq, gq, wkv, gkn, gkr, rope_c, rope_s, tm=tm_c, **shape_kw)
            mla_ckv.append(ckv_n.reshape(batch, seq, KV_LORA))
            mla_kr.append(kr.reshape(batch, seq, ROPE))
            kx, vx = _cache_kv(
                cache_mla_ckv[:, i].reshape(dbatch * past, KV_LORA),
                jnp.pad(cache_mla_krope[:, i].reshape(dbatch * past, ROPE), ((0, 0), (0, LANES - ROPE))),
                wkv, gkn, gkr, ident_c, jnp.zeros((past, LANES), F32), tm=past)
            ya_c = _ctx_attn(q, k, v, (0, 0, 0), nc=nc, batch=batch, seq=seq,
                             heads=H_C, dq=QK_PAD, dv=V_DIM)
            ya_s = _mla_attn(q, k, v, kx, vx, dbatch=dbatch, past=past,
                             tq=_pick_tile(256, dseq), **shape_kw)
            x, h2 = _outproj(x, x2, second, mods, norm2_g[l][None, :], ya_c, ya_s, w_o_c_b, i,
                             seq=seq, tm=tm, **shape_kw)
        last = l == depth - 1
        res = _ffn(x, h2, mods, w_ffn_b[0], w_ffn_b[1], None if last else (w_ffn_in, w_ffn_out), l,
                   last=last, tm=tm if last else tm_f, tf=tf, **shape_kw)
        if last:
            x2 = res
        else:
            x2, second, w_ffn_b = res[0], "add", res[1:]

    return (x2[0].reshape(batch, seq, d), x2[1].reshape(dbatch, dseq, d), nat[0], nat[1],
            jnp.stack(mla_ckv, axis=1), jnp.stack(mla_kr, axis=1))
```

```python
import functools

import numpy as np
import jax
import jax.numpy as jnp
from jax import lax
from jax.experimental import pallas as pl
from jax.experimental.pallas import tpu as pltpu

F32 = jnp.float32
BF16 = jnp.bfloat16

EPS = 1e-6
GRID_W = 64
WIN_R = 8
WIN_C = 16
ROPE_BASE = 10000.0
H_A = 8
DH_A = 128
H_C = 16
Q_LORA = 512
KV_LORA = 256
NOPE = 128
ROPE = 64
V_DIM = 128
QK_DIM = NOPE + ROPE
QK_PAD = 256
LANES = 128
SUBLANES = 8
HALO = 16
NEG = -1e30
LOG2E = 1.4426950408889634

VMEM_LIMIT = 56 * 1024 * 1024
ROW_BLOCK = 128
ROW_BLOCK_AB = 256


def _params(*sem):
    return pltpu.CompilerParams(dimension_semantics=sem, vmem_limit_bytes=VMEM_LIMIT)


def _group(i, nct, per):
    return jnp.where(i < nct, 0, 1 + (i - nct) // per)


def _dot(a, b):
    return jnp.dot(a, b, preferred_element_type=F32)


def _dot_nt(a, b):
    return lax.dot_general(a, b, (((1,), (1,)), ((), ())), preferred_element_type=F32)


def _ada_kernel(c_ref, w_ref, b_ref, o_ref):
    c = c_ref[...]
    s = (c / (1.0 + jnp.exp(-c))).astype(BF16)
    o_ref[...] = _dot(s, w_ref[...].astype(BF16)) + b_ref[...]


def _adaln(cond, w_ada, b_ada):
    depth, d, n6 = w_ada.shape
    gp = cond.shape[0]
    tn = 1024 if n6 % 1024 == 0 else n6
    return pl.pallas_call(
        _ada_kernel,
        out_shape=jax.ShapeDtypeStruct((depth, gp, n6), F32),
        grid=(depth, n6 // tn),
        in_specs=[
            pl.BlockSpec((gp, d), lambda l, n: (0, 0)),
            pl.BlockSpec((None, d, tn), lambda l, n: (l, 0, n)),
            pl.BlockSpec((None, 1, tn), lambda l, n: (l, 0, n)),
        ],
        out_specs=pl.BlockSpec((None, gp, tn), lambda l, n: (l, 0, n)),
        compiler_params=_params("arbitrary", "arbitrary"),
        name="adaln",
    )(cond, w_ada, b_ada.reshape(depth, 1, n6))


def _row_spec(tm, d, nct, second, which):
    if second != "select":
        return pl.BlockSpec((tm, d), lambda i, *_: (i, 0))
    if which == 0:
        return pl.BlockSpec((tm, d), lambda i, *_: (jnp.minimum(i, nct - 1), 0))
    return pl.BlockSpec((tm, d), lambda i, *_: (jnp.maximum(i - nct, 0), 0))


def _residual(x_ref, x2_ref, rows, second, is_ctx):
    x = x_ref[rows, :]
    if second == "add":
        x = x + x2_ref[rows, :]
    elif second == "select":
        x = jnp.where(is_ctx, x, x2_ref[rows, :])
    return x


def _inproj_ab_kernel(x_ref, *rest, nct, seq, first, rb, second):
    x2_ref = None
    if second:
        x2_ref, *rest = rest
    mod_ref, g1_ref, w_ref, gq_ref, gk_ref, *rest = rest
    if first:
        qkv_ref, gb_ref, z_ref, natk_ref, natv_ref = rest
    else:
        _, _, qkv_ref, gb_ref, z_ref, natk_ref, natv_ref = rest
    i = pl.program_id(0)
    tm = x_ref.shape[0]
    na = H_A * DH_A
    is_ctx = i < nct
    gs = g1_ref[...] * (1.0 + mod_ref[1:2, :])

    def nat_store(nat_ref, rows, hh, val):
        bb, off = rows.start // seq, rows.start % seq
        if first:
            nat_ref[bb, 0, hh, off:off + rb, :] = val
        else:
            nat_ref[bb, hh, off:off + rb, :] = val

    def head_norm(rows, r, col, gain_ref, nat_ref):
        for hh in range(H_A):
            t = r[:, hh * DH_A:(hh + 1) * DH_A]
            ms = jnp.mean(t * t, axis=-1, keepdims=True)
            val = t * lax.rsqrt(ms + EPS) * gain_ref[...]
            qkv_ref[rows, col * na + hh * DH_A:col * na + (hh + 1) * DH_A] = val.astype(BF16)
            if nat_ref is not None:
                nat_store(nat_ref, rows, hh, val)

    def tile(ctx):
        if ctx and first and natk_ref.shape[1] > 1:
            natk_ref[:, 1:] = jnp.zeros_like(natk_ref[:, 1:])
            natv_ref[:, 1:] = jnp.zeros_like(natv_ref[:, 1:])
        for r0 in range(0, tm, rb):
            rows = slice(r0, r0 + rb)
            x = _residual(x_ref, x2_ref, rows, second, is_ctx)
            ms = jnp.mean(x * x, axis=-1, keepdims=True)
            h = (x * lax.rsqrt(ms + EPS) * gs + mod_ref[0:1, :]).astype(BF16)

            def proj(col):
                return _dot(h, w_ref[:, col * na:(col + 1) * na])

            head_norm(rows, proj(0), 0, gq_ref, None)
            head_norm(rows, proj(1), 1, gk_ref, natk_ref if ctx else None)
            v = proj(2)
            qkv_ref[rows, 2 * na:3 * na] = v.astype(BF16)
            if ctx:
                for hh in range(H_A):
                    nat_store(natv_ref, rows, hh, v[:, hh * DH_A:(hh + 1) * DH_A])
            gb_ref[rows, :] = proj(3).astype(BF16)
            gc = proj(4)
            z_ref[rows, :] = (gc * proj(5)).astype(BF16)

    pl.when(is_ctx)(lambda: tile(True))
    pl.when(jnp.logical_not(is_ctx))(lambda: tile(False))


def _inproj_ab(x, x2, second, mods, g1, w_in, gq, gk, nat_prev, slot,
               *, nc, seq, dseq, batch, n_even, tm):
    d = x.shape[1]
    n = nc + (x2.shape[0] if second == "select" else x.shape[0] - nc)
    na = H_A * DH_A
    nct = nc // tm
    per = dseq // tm
    first = nat_prev is None
    kern = functools.partial(_inproj_ab_kernel, nct=nct, seq=seq, first=first,
                             rb=_pick_tile(ROW_BLOCK_AB, tm, seq), second=second)
    nat_shape = jax.ShapeDtypeStruct((batch, n_even, H_A, seq, DH_A), F32)
    if first:
        nat_spec = pl.BlockSpec((tm // seq, n_even, H_A, seq, DH_A),
                                lambda i: (jnp.minimum(i, nct - 1), 0, 0, 0, 0))
    else:
        nat_spec = pl.BlockSpec((tm // seq, None, H_A, seq, DH_A),
                                lambda i: (jnp.minimum(i, nct - 1), slot, 0, 0, 0))
    in_specs = [_row_spec(tm, d, nct, second, 0)] + ([_row_spec(tm, d, nct, second, 1)] if second else []) + [
        pl.BlockSpec((None, 6, d), lambda i: (_group(i, nct, per), 0, 0)),
        pl.BlockSpec((1, d), lambda i: (0, 0)),
        pl.BlockSpec((None,) + w_in.shape[1:], lambda i: (slot, 0, 0), pipeline_mode=pl.Buffered(1)),
        pl.BlockSpec((1, DH_A), lambda i: (0, 0)),
        pl.BlockSpec((1, DH_A), lambda i: (0, 0)),
    ]
    args = [x] + ([x2] if second else []) + [mods, g1, w_in, gq, gk]
    aliases = {}
    if not first:
        in_specs += [pl.BlockSpec(memory_space=pl.ANY)] * 2
        aliases = {len(args): 3, len(args) + 1: 4}
        args += list(nat_prev)
    return pl.pallas_call(
        kern,
        out_shape=(jax.ShapeDtypeStruct((n, 3 * na), BF16),
                   jax.ShapeDtypeStruct((n, na), BF16),
                   jax.ShapeDtypeStruct((n, na), BF16),
                   nat_shape, nat_shape),
        grid=(n // tm,),
        in_specs=in_specs,
        out_specs=(
            pl.BlockSpec((tm, 3 * na), lambda i: (i, 0)),
            pl.BlockSpec((tm, na), lambda i: (i, 0)),
            pl.BlockSpec((tm, na), lambda i: (i, 0)),
            nat_spec, nat_spec,
        ),
        input_output_aliases=aliases,
        compiler_params=_params("arbitrary"),
        name="inproj_ab",
    )(*args)


def _ctx_attn_kernel(q_ref, k_ref, v_ref, o_ref, *, heads, dq, dv):
    def scores(hh):
        return _dot_nt(q_ref[:, hh * dq:(hh + 1) * dq], k_ref[:, hh * dq:(hh + 1) * dq])

    s_next = scores(0)
    for hh in range(heads):
        s = s_next
        if hh + 1 < heads:
            s_next = scores(hh + 1)
        m = jnp.max(s, axis=-1, keepdims=True)
        p = jnp.exp2(s - m)
        l = jnp.sum(p, axis=-1, keepdims=True)
        o = _dot(p.astype(BF16), v_ref[:, hh * dv:(hh + 1) * dv]) / l
        o_ref[:, hh * dv:(hh + 1) * dv] = o.astype(BF16)


def _ctx_attn(q_arr, k_arr, v_arr, cols, *, nc, batch, seq, heads, dq, dv):
    cq, ck, cv = cols
    kern = functools.partial(_ctx_attn_kernel, heads=heads, dq=dq, dv=dv)
    return pl.pallas_call(
        kern,
        out_shape=jax.ShapeDtypeStruct((nc, heads * dv), BF16),
        grid=(batch,),
        in_specs=[
            pl.BlockSpec((seq, heads * dq), lambda b: (b, cq)),
            pl.BlockSpec((seq, heads * dq), lambda b: (b, ck)),
            pl.BlockSpec((seq, heads * dv), lambda b: (b, cv)),
        ],
        out_specs=pl.BlockSpec((seq, heads * dv), lambda b: (b, 0)),
        compiler_params=_params("arbitrary"),
        name="ctx_attn",
    )(q_arr, k_arr, v_arr)


NAT_QR = 8
NAT_KR = 16


def _nat_window(a, rows):
    return int(np.clip(a * NAT_QR - WIN_R // 2, 0, rows - NAT_KR))


def _nat_attn_kernel(rpb_ref, q_ref, k_ref, v_ref, ck_ref, cv_ref, o_ref,
                     t2_ref, bias_ref, *, rows):
    hh = pl.program_id(0)
    b = pl.program_id(1)
    nblk = rows // NAT_QR
    n_dr = 2 * WIN_R - 1
    n_dc = 2 * WIN_C - 1

    @pl.when(b == 0)
    def _():
        j = lax.broadcasted_iota(jnp.int32, (GRID_W, LANES), 0)
        kc = lax.broadcasted_iota(jnp.int32, (GRID_W, LANES), 1) % GRID_W
        cs = jnp.clip(j - WIN_C // 2, 0, GRID_W - WIN_C)
        col_in = (kc >= cs) & (kc < cs + WIN_C)
        dc = kc - j + (WIN_C - 1)
        for d in range(n_dr):
            t = jnp.full((GRID_W, LANES), NEG, F32)
            for c in range(n_dc):
                t = jnp.where(col_in & (dc == c), rpb_ref[hh, d * n_dc + c] * LOG2E, t)
            t2_ref[d] = t
        lane = lax.broadcasted_iota(jnp.int32, (GRID_W, LANES), 1)
        negt = jnp.full((GRID_W, LANES), NEG, F32)
        for a in range(nblk):
            ws = _nat_window(a, rows)
            for qr in range(NAT_QR):
                r = a * NAT_QR + qr
                rs = int(np.clip(r - WIN_R // 2, 0, rows - WIN_R))
                for pp in range(NAT_KR // 2):
                    halves = []
                    for kk in (2 * pp, 2 * pp + 1):
                        kr = ws + kk
                        halves.append(t2_ref[kr - r + WIN_R - 1] if rs <= kr < rs + WIN_R else None)
                    if halves[0] is None and halves[1] is None:
                        blk = negt
                    else:
                        left = negt if halves[0] is None else halves[0]
                        right = negt if halves[1] is None else halves[1]
                        blk = jnp.where(lane < GRID_W, left, right)
                    bias_ref[a, qr * GRID_W:(qr + 1) * GRID_W, pp * LANES:(pp + 1) * LANES] = blk

    ck = ck_ref[...].astype(BF16)
    cv = cv_ref[...].astype(BF16)
    tq = NAT_QR * GRID_W
    tk = NAT_KR * GRID_W

    def scores(a):
        ws = _nat_window(a, rows)
        q = q_ref[a * tq:(a + 1) * tq, :]
        kw = k_ref[ws * GRID_W:ws * GRID_W + tk, :]
        return _dot_nt(q, kw) + bias_ref[a], _dot_nt(q, ck)

    s_next = scores(0)
    for a in range(nblk):
        s_loc, s_ctx = s_next
        if a + 1 < nblk:
            s_next = scores(a + 1)
        ws = _nat_window(a, rows)
        vw = v_ref[ws * GRID_W:ws * GRID_W + tk, :]
        m = jnp.maximum(jnp.max(s_loc, axis=-1, keepdims=True),
                        jnp.max(s_ctx, axis=-1, keepdims=True))
        p_loc = jnp.exp2(s_loc - m)
        p_ctx = jnp.exp2(s_ctx - m)
        l = jnp.sum(p_loc, axis=-1, keepdims=True) + jnp.sum(p_ctx, axis=-1, keepdims=True)
        o = (_dot(p_loc.astype(BF16), vw) + _dot(p_ctx.astype(BF16), cv)) / l
        o_ref[a * tq:(a + 1) * tq, :] = o.astype(BF16)


def _nat_attn(rpb, qkv, cache_k, cache_v, slot, *, nc, dseq, dbatch):
    rows = dseq // GRID_W
    assert rows % NAT_QR == 0 and rows >= NAT_KR
    nblk = rows // NAT_QR
    past = cache_k.shape[3]
    off = nc // dseq
    n_dr, n_dc = 2 * WIN_R - 1, 2 * WIN_C - 1
    kern = functools.partial(_nat_attn_kernel, rows=rows)
    cache_spec = pl.BlockSpec((None, None, None, past, DH_A), lambda h, b: (b, slot, h, 0, 0))
    return pl.pallas_call(
        kern,
        out_shape=jax.ShapeDtypeStruct((dbatch * dseq, H_A * DH_A), BF16),
        grid=(H_A, dbatch),
        in_specs=[
            pl.BlockSpec(memory_space=pltpu.SMEM),
            pl.BlockSpec((dseq, DH_A), lambda h, b: (off + b, h)),
            pl.BlockSpec((dseq, DH_A), lambda h, b: (off + b, H_A + h)),
            pl.BlockSpec((dseq, DH_A), lambda h, b: (off + b, 2 * H_A + h)),
            cache_spec, cache_spec,
        ],
        out_specs=pl.BlockSpec((dseq, DH_A), lambda h, b: (b, h)),
        scratch_shapes=[pltpu.VMEM((n_dr, GRID_W, LANES), F32),
                        pltpu.VMEM((nblk, NAT_QR * GRID_W, NAT_KR * GRID_W), F32)],
        compiler_params=_params("arbitrary", "arbitrary"),
        name="nat_attn",
    )(rpb.reshape(H_A, n_dr * n_dc), qkv, qkv, qkv, cache_k, cache_v)


def _outproj_kernel(x_ref, *refs, conv, nct, seq, dseq, rb, second):
    x2_ref = None
    if second:
        x2_ref, *refs = refs
    if conv:
        (mod_ref, g2_ref, yac_ref, yas_ref, gb_ref, z_ref, zp_ref, zn_ref, cw_ref, w_ref,
         o_ref, h2_ref, zs_ref) = refs
    else:
        mod_ref, g2_ref, yac_ref, yas_ref, w_ref, o_ref, h2_ref = refs
    i = pl.program_id(0)
    tm = x_ref.shape[0]
    ka = yac_ref.shape[1]
    if conv:
        zs_ref[0:HALO, :] = zp_ref[...].astype(F32)
        zs_ref[HALO:HALO + tm, :] = z_ref[...].astype(F32)
        zs_ref[HALO + tm:2 * HALO + tm, :] = zn_ref[...].astype(F32)
        last = jnp.where(i < nct, seq - 1, dseq - 1)
    gate = mod_ref[2:3, :]
    gs = g2_ref[...] * (1.0 + mod_ref[4:5, :])
    shift = mod_ref[3:4, :]
    for r0 in range(0, tm, rb):
        rows = slice(r0, r0 + rb)
        ya = jnp.where(i < nct, yac_ref[rows, :], yas_ref[rows, :])
        y = _dot(ya, w_ref[0:ka, :])
        if conv:
            row = r0 + lax.broadcasted_iota(jnp.int32, (rb, 1), 0)
            pos = jnp.where(i < nct, row % seq, ((i - nct) * tm + row) % dseq)
            z_prev = jnp.where(pos == 0, 0.0, zs_ref[HALO - 1 + r0:HALO - 1 + r0 + rb, :])
            z_next = jnp.where(pos == last, 0.0, zs_ref[HALO + 1 + r0:HALO + 1 + r0 + rb, :])
            z_mid = zs_ref[HALO + r0:HALO + r0 + rb, :]
            cv = cw_ref[0:1, :] * z_prev + cw_ref[1:2, :] * z_mid + cw_ref[2:3, :] * z_next
            yb = gb_ref[rows, :].astype(F32) * cv
            y = y + _dot(yb.astype(BF16), w_ref[ka:, :])
        xn = _residual(x_ref, x2_ref, rows, second, i < nct) + gate * y
        o_ref[rows, :] = xn
        ms = jnp.mean(xn * xn, axis=-1, keepdims=True)
        h2_ref[rows, :] = (xn * lax.rsqrt(ms + EPS) * gs + shift).astype(BF16)


def _outproj(x, x2, second, mods, g2, ya_c, ya_s, w_out, slot, conv_in=None, *, nc, seq, dseq, tm):
    d = x.shape[1]
    n = nc + ya_s.shape[0]
    ka = ya_c.shape[1]
    nct = nc // tm
    per = dseq // tm
    conv = conv_in is not None
    kern = functools.partial(_outproj_kernel, conv=conv, nct=nct, seq=seq, dseq=dseq,
                             rb=_pick_tile(ROW_BLOCK, tm), second=second)
    in_specs = [_row_spec(tm, d, nct, second, 0)] + ([_row_spec(tm, d, nct, second, 1)] if second else []) + [
        pl.BlockSpec((None, 6, d), lambda i: (_group(i, nct, per), 0, 0)),
        pl.BlockSpec((1, d), lambda i: (0, 0)),
        pl.BlockSpec((tm, ka), lambda i: (jnp.minimum(i, nct - 1), 0)),
        pl.BlockSpec((tm, ka), lambda i: (jnp.maximum(i - nct, 0), 0)),
    ]
    args = [x] + ([x2] if second else []) + [mods, g2, ya_c, ya_s]
    scratch = []
    if conv:
        gb, z, cw = conv_in
        nb = z.shape[1]
        th = tm // HALO
        nh = n // HALO
        in_specs += [
            pl.BlockSpec((tm, nb), lambda i: (i, 0)),
            pl.BlockSpec((tm, nb), lambda i: (i, 0)),
            pl.BlockSpec((HALO, nb), lambda i: (jnp.maximum(i * th - 1, 0), 0)),
            pl.BlockSpec((HALO, nb), lambda i: (jnp.minimum((i + 1) * th, nh - 1), 0)),
            pl.BlockSpec((3, nb), lambda i: (0, 0)),
        ]
        args += [gb, z, z, z, cw]
        scratch = [pltpu.VMEM((tm + 2 * HALO, nb), F32)]
    in_specs.append(pl.BlockSpec((None,) + w_out.shape[1:], lambda i: (slot, 0, 0),
                                 pipeline_mode=pl.Buffered(1)))
    args.append(w_out)
    return pl.pallas_call(
        kern,
        out_shape=(jax.ShapeDtypeStruct((n, d), F32), jax.ShapeDtypeStruct((n, d), BF16)),
        grid=(n // tm,),
        in_specs=in_specs,
        out_specs=(pl.BlockSpec((tm, d), lambda i: (i, 0)), pl.BlockSpec((tm, d), lambda i: (i, 0))),
        scratch_shapes=scratch,
        compiler_params=_params("arbitrary"),
        name="outproj_conv" if conv else "outproj",
    )(*args)


def _ffn_kernel(*refs, last, nct, cast_next):
    if last:
        x_ref, h_ref, mod_ref, wg_ref, wu_ref, wo_ref, oc_ref, os_ref, acc_ref = refs
    elif cast_next:
        (h_ref, mod_ref, wg_ref, wu_ref, wo_ref, nin_ref, nout_ref,
         acc_ref, nin_b_ref, nout_b_ref) = refs
    else:
        h_ref, mod_ref, wg_ref, wu_ref, wo_ref, acc_ref = refs
    i = pl.program_id(0)
    f = pl.program_id(1)

    def step(start):
        h = h_ref[...]
        g = _dot(h, wg_ref[...])
        u = _dot(h, wu_ref[...])
        a = (g / (1.0 + jnp.exp(-g))) * u
        part = _dot(a.astype(BF16), wo_ref[...])
        acc_ref[...] = part if start else acc_ref[...] + part
        if cast_next:
            nin_b_ref[...] = nin_ref[...].astype(BF16)
            nout_b_ref[...] = nout_ref[...].astype(BF16)

    pl.when(f == 0)(lambda: step(True))
    pl.when(f != 0)(lambda: step(False))

    @pl.when(f == pl.num_programs(1) - 1)
    def _():
        delta = mod_ref[5:6, :] * acc_ref[...]
        if last:
            @pl.when(i < nct)
            def _():
                oc_ref[...] = x_ref[...] + delta

            @pl.when(i >= nct)
            def _():
                os_ref[...] = x_ref[...] + delta
        else:
            acc_ref[...] = delta


def _slab_rows(total, steps):
    r = HALO
    while total % r or total // r > steps:
        r += HALO
    return r


def _ffn(x, h2, mods, w_in, w_out, next_w, layer, *, last, nc, dseq, tm, tf):
    n, d = h2.shape
    dff = w_out.shape[0]
    nf = dff // tf
    nct = nc // tm
    per = dseq // tm
    cast_next = next_w is not None
    kern = functools.partial(_ffn_kernel, last=last, nct=nct, cast_next=cast_next)
    row_spec = pl.BlockSpec((tm, d), lambda i, f: (i, 0))
    in_specs = [
        row_spec,
        pl.BlockSpec((None, 6, d), lambda i, f: (_group(i, nct, per), 0, 0)),
        pl.BlockSpec((d, tf), lambda i, f: (0, f)),
        pl.BlockSpec((d, tf), lambda i, f: (0, nf + f)),
        pl.BlockSpec((tf, d), lambda i, f: (f, 0)),
    ]
    args = [h2, mods, w_in, w_in, w_out]
    scratch = []
    if last:
        in_specs.insert(0, row_spec)
        args.insert(0, x)
        out_shape = (jax.ShapeDtypeStruct((nc, d), F32), jax.ShapeDtypeStruct((n - nc, d), F32))
        out_specs = (pl.BlockSpec((tm, d), lambda i, f: (jnp.minimum(i, nct - 1), 0)),
                     pl.BlockSpec((tm, d), lambda i, f: (jnp.maximum(i - nct, 0), 0)))
        scratch = [pltpu.VMEM((tm, d), F32)]
    else:
        out_shape = jax.ShapeDtypeStruct((n, d), F32)
        out_specs = row_spec
    if cast_next:
        steps = (n // tm) * nf
        r_in, r_out = _slab_rows(d, steps), _slab_rows(dff, steps)
        slab = lambda nslab: (lambda i, f: (jnp.minimum(i * nf + f, nslab - 1), 0))
        slab3 = lambda nslab: (lambda i, f: (layer + 1, jnp.minimum(i * nf + f, nslab - 1), 0))
        in_specs += [pl.BlockSpec((None, r_in, 2 * dff), slab3(d // r_in)),
                     pl.BlockSpec((None, r_out, d), slab3(dff // r_out))]
        args += list(next_w)
        out_shape = (out_shape, jax.ShapeDtypeStruct((d, 2 * dff), BF16),
                     jax.ShapeDtypeStruct((dff, d), BF16))
        out_specs = (out_specs, pl.BlockSpec((r_in, 2 * dff), slab(d // r_in)),
                     pl.BlockSpec((r_out, d), slab(dff // r_out)))
    return pl.pallas_call(
        kern,
        out_shape=out_shape,
        grid=(n // tm, nf),
        in_specs=in_specs,
        out_specs=out_specs,
        scratch_shapes=scratch,
        compiler_params=_params("arbitrary", "arbitrary"),
        name="ffn",
    )(*args)


def _rope_low_half(rows):
    lane = lax.broadcasted_iota(jnp.int32, (rows, LANES), 1)
    return lane % (ROPE // 2) < ROPE // 4


def _rope(x, c, s, low):
    quarter = ROPE // 4
    swapped = jnp.where(low, pltpu.roll(x, LANES - quarter, 1), pltpu.roll(x, quarter, 1))
    return x * c + swapped * s


def _mla_kv_expand(ckv_bf16, krp, wkv_ref, gkn_ref, gkr_ref, c, s, low, k_ref, v_ref, rows=slice(None)):
    kv = _dot(ckv_bf16, wkv_ref[...])
    krr = _rope(krp * gkr_ref[...], c, s, low)
    ss_kr = jnp.sum(krp * krp, axis=-1, keepdims=True)
    for hh in range(H_C):
        base = hh * (NOPE + V_DIM)
        kn = kv[:, base:base + NOPE]
        rstd = lax.rsqrt((jnp.sum(kn * kn, axis=-1, keepdims=True) + ss_kr) / QK_DIM + EPS)
        k_ref[rows, hh * QK_PAD:hh * QK_PAD + NOPE] = (kn * rstd * gkn_ref[...]).astype(BF16)
        k_ref[rows, hh * QK_PAD + NOPE:(hh + 1) * QK_PAD] = (krr * rstd).astype(BF16)
        v_ref[rows, hh * V_DIM:(hh + 1) * V_DIM] = kv[:, base + NOPE:base + NOPE + V_DIM].astype(BF16)


def _inproj_c_kernel(x_ref, *rest, nct, rb, second):
    x2_ref = None
    if second:
        x2_ref, *rest = rest
    (mod_ref, g1_ref, wd_ref, gcq_ref, gckv_ref, wq_ref, gq_ref, wkv_ref, gkn_ref, gkr_ref,
     c_ref, s_ref, q_ref, k_ref, v_ref, ckv_ref, kr_ref) = rest
    i = pl.program_id(0)
    tm = x_ref.shape[0]
    blocks = [slice(r * rb, (r + 1) * rb) for r in range(tm // rb)]
    gs = g1_ref[...] * (1.0 + mod_ref[1:2, :])
    low = _rope_low_half(rb)

    def latents(rows):
        x = _residual(x_ref, x2_ref, rows, second, i < nct)
        ms = jnp.mean(x * x, axis=-1, keepdims=True)
        h = (x * lax.rsqrt(ms + EPS) * gs + mod_ref[0:1, :]).astype(BF16)
        dn = _dot(h, wd_ref[...])
        cq = dn[:, :Q_LORA]
        ckv = dn[:, Q_LORA:Q_LORA + KV_LORA]
        krp = dn[:, Q_LORA + KV_LORA:]
        cqn = cq * lax.rsqrt(jnp.mean(cq * cq, axis=-1, keepdims=True) + EPS) * gcq_ref[...]
        ckvn = ckv * lax.rsqrt(jnp.mean(ckv * ckv, axis=-1, keepdims=True) + EPS) * gckv_ref[...]
        return cqn.astype(BF16), ckvn, krp

    def expand(rows, cqn, ckvn, krp, ctx):
        if ctx:
            ckv_ref[rows, :] = ckvn
            kr_ref[rows, :] = krp[:, :ROPE]
        c = c_ref[rows, :]
        s = s_ref[rows, :]
        q = _dot(cqn, wq_ref[...])
        for hh in range(H_C):
            qh = q[:, hh * QK_PAD:(hh + 1) * QK_PAD]
            rstd = lax.rsqrt(jnp.sum(qh * qh, axis=-1, keepdims=True) / QK_DIM + EPS)
            qn = qh * rstd * gq_ref[...]
            q_ref[rows, hh * QK_PAD:hh * QK_PAD + NOPE] = qn[:, :NOPE].astype(BF16)
            q_ref[rows, hh * QK_PAD + NOPE:(hh + 1) * QK_PAD] = _rope(qn[:, NOPE:], c, s, low).astype(BF16)
        _mla_kv_expand(ckvn.astype(BF16), krp, wkv_ref, gkn_ref, gkr_ref, c, s, low, k_ref, v_ref, rows)

    def tile(ctx):
        nxt = latents(blocks[0])
        for r, rows in enumerate(blocks):
            cur = nxt
            if r + 1 < len(blocks):
                nxt = latents(blocks[r + 1])
            expand(rows, *cur, ctx)

    pl.when(i < nct)(lambda: tile(True))
    pl.when(i >= nct)(lambda: tile(False))


def _inproj_c(x, x2, second, mods, g1, wd, gcq, gckv, wq, gq, wkv, gkn, gkr, rope_c, rope_s,
              *, nc, dseq, tm):
    d = x.shape[1]
    n = nc + (x2.shape[0] if second == "select" else x.shape[0] - nc)
    nct = nc // tm
    per = dseq // tm
    kern = functools.partial(_inproj_c_kernel, nct=nct, rb=_pick_tile(ROW_BLOCK, tm), second=second)
    const = lambda shape: pl.BlockSpec(shape, lambda i: (0,) * len(shape),
                                       pipeline_mode=pl.Buffered(1))
    rope_idx = lambda i: (jnp.where(i < nct, 0, 1 + (i - nct) % per), 0)
    ctx_idx = lambda i: (jnp.minimum(i, nct - 1), 0)
    return pl.pallas_call(
        kern,
        out_shape=(jax.ShapeDtypeStruct((n, H_C * QK_PAD), BF16),
                   jax.ShapeDtypeStruct((n, H_C * QK_PAD), BF16),
                   jax.ShapeDtypeStruct((n, H_C * V_DIM), BF16),
                   jax.ShapeDtypeStruct((nc, KV_LORA), F32),
                   jax.ShapeDtypeStruct((nc, ROPE), F32)),
        grid=(n // tm,),
        in_specs=[_row_spec(tm, d, nct, second, 0)] + ([_row_spec(tm, d, nct, second, 1)] if second else []) + [
            pl.BlockSpec((None, 6, d), lambda i: (_group(i, nct, per), 0, 0)),
            const((1, d)), const(wd.shape), const((1, Q_LORA)), const((1, KV_LORA)),
            const(wq.shape), const((1, QK_PAD)), const(wkv.shape),
            const((1, NOPE)), const((1, LANES)),
            pl.BlockSpec((tm, LANES), rope_idx),
            pl.BlockSpec((tm, LANES), rope_idx),
        ],
        out_specs=(
            pl.BlockSpec((tm, H_C * QK_PAD), lambda i: (i, 0)),
            pl.BlockSpec((tm, H_C * QK_PAD), lambda i: (i, 0)),
            pl.BlockSpec((tm, H_C * V_DIM), lambda i: (i, 0)),
            pl.BlockSpec((tm, KV_LORA), ctx_idx),
            pl.BlockSpec((tm, ROPE), ctx_idx),
        ),
        compiler_params=_params("arbitrary"),
        name="inproj_c",
    )(*([x] + ([x2] if second else []) + [mods, g1, wd, gcq, gckv, wq, gq, wkv, gkn, gkr, rope_c, rope_s]))


def _cache_kv_kernel(ckv_ref, krp_ref, wkv_ref, gkn_ref, gkr_ref, c_ref, s_ref, k_ref, v_ref):
    _mla_kv_expand(ckv_ref[...].astype(BF16), krp_ref[...], wkv_ref, gkn_ref, gkr_ref,
                   c_ref[...], s_ref[...], _rope_low_half(ckv_ref.shape[0]), k_ref, v_ref)


def _cache_kv(ckv, krp, wkv, gkn, gkr, rope_c, rope_s, *, tm):
    n = ckv.shape[0]
    const = lambda shape: pl.BlockSpec(shape, lambda i: (0,) * len(shape))
    return pl.pallas_call(
        _cache_kv_kernel,
        out_shape=(jax.ShapeDtypeStruct((n, H_C * QK_PAD), BF16),
                   jax.ShapeDtypeStruct((n, H_C * V_DIM), BF16)),
        grid=(n // tm,),
        in_specs=[
            pl.BlockSpec((tm, KV_LORA), lambda i: (i, 0)),
            pl.BlockSpec((tm, LANES), lambda i: (i, 0)),
            const(wkv.shape), const((1, NOPE)), const((1, LANES)),
            const((tm, LANES)), const((tm, LANES)),
        ],
        out_specs=(pl.BlockSpec((tm, H_C * QK_PAD), lambda i: (i, 0)),
                   pl.BlockSpec((tm, H_C * V_DIM), lambda i: (i, 0))),
        compiler_params=_params("arbitrary"),
        name="cache_kv",
    )(ckv, krp, wkv, gkn, gkr, rope_c, rope_s)


def _mla_attn_kernel(q_ref, k_ref, v_ref, kx_ref, vx_ref, o_ref, kall_ref, vt_ref, *, tq):
    past = kx_ref.shape[0]
    kall_ref[0:past, :] = kx_ref[...]
    kall_ref[past:, :] = k_ref[...]
    vt_ref[:, 0:past] = vx_ref[...].astype(F32).T.astype(BF16)
    vt_ref[:, past:] = v_ref[...].astype(F32).T.astype(BF16)

    def scores(t):
        q = q_ref[t * tq:(t + 1) * tq, :]
        return _dot_nt(kall_ref[...], q)

    nblk = q_ref.shape[0] // tq
    s_next = scores(0)
    for t in range(nblk):
        s = s_next
        if t + 1 < nblk:
            s_next = scores(t + 1)
        m = jnp.max(s, axis=0, keepdims=True)
        p = jnp.exp2(s - m)
        l = jnp.sum(p, axis=0, keepdims=True)
        ot = _dot(vt_ref[...], p.astype(BF16)) / l
        o_ref[t * tq:(t + 1) * tq, :] = ot.T.astype(BF16)


def _mla_attn(q, k, v, kx, vx, *, nc, dseq, dbatch, past, tq):
    off = nc // dseq
    kern = functools.partial(_mla_attn_kernel, tq=tq)
    return pl.pallas_call(
        kern,
        out_shape=jax.ShapeDtypeStruct((dbatch * dseq, H_C * V_DIM), BF16),
        grid=(dbatch, H_C),
        in_specs=[
            pl.BlockSpec((dseq, QK_PAD), lambda b, h: (off + b, h)),
            pl.BlockSpec((dseq, QK_PAD), lambda b, h: (off + b, h)),
            pl.BlockSpec((dseq, V_DIM), lambda b, h: (off + b, h)),
            pl.BlockSpec((past, QK_PAD), lambda b, h: (b, h)),
            pl.BlockSpec((past, V_DIM), lambda b, h: (b, h)),
        ],
        out_specs=pl.BlockSpec((dseq, V_DIM), lambda b, h: (b, h)),
        scratch_shapes=[pltpu.VMEM((past + dseq, QK_PAD), BF16),
                        pltpu.VMEM((V_DIM, past + dseq), BF16)],
        compiler_params=_params("arbitrary", "arbitrary"),
        name="mla_attn",
    )(q, k, v, kx, vx)


def _rope_tables(dseq, tm):
    t = np.arange(dseq)
    quarter = ROPE // 4
    freqs = jnp.asarray(ROPE_BASE, F32) ** (-jnp.arange(quarter, dtype=F32) / quarter)
    ang_r = jnp.asarray(t // GRID_W, F32)[:, None] * freqs
    ang_c = jnp.asarray(t % GRID_W, F32)[:, None] * freqs
    zeros = jnp.zeros((dseq, LANES - ROPE), F32)
    cos = jnp.concatenate([jnp.cos(ang_r), jnp.cos(ang_r), jnp.cos(ang_c), jnp.cos(ang_c), zeros], axis=1)
    sin = jnp.concatenate([-jnp.sin(ang_r), jnp.sin(ang_r), -jnp.sin(ang_c), jnp.sin(ang_c), zeros], axis=1)
    ident = jnp.concatenate([jnp.ones((tm, ROPE), F32), jnp.zeros((tm, LANES - ROPE), F32)], axis=1)
    return (jnp.concatenate([ident, cos], axis=0),
            jnp.concatenate([jnp.zeros((tm, LANES), F32), sin], axis=0))


def _pick_tile(pref, *lengths):
    t = pref
    while any(l % t for l in lengths):
        t //= 2
    return t


def kernel(x_prompt, x_sample, cache_nat_k, cache_nat_v, cache_mla_ckv, cache_mla_krope, c, c_ctx,
           norm1_g, norm2_g, w_ada, b_ada, w_in_ab, g_qn_a, g_kn_a, rpb_a, conv_b_w, w_out_ab,
           w_down_c, g_cq, g_ckv, w_uq_c, w_ukv_c, g_qn_c, g_kn_c, w_o_c, w_ffn_in, w_ffn_out):
    batch, seq, d = x_prompt.shape
    dbatch, dseq, _ = x_sample.shape
    depth = w_ada.shape[0]
    n_even = w_in_ab.shape[0]
    past = cache_nat_k.shape[3]
    nc, ns = batch * seq, dbatch * dseq
    n = nc + ns
    dff = w_ffn_out.shape[1]
    assert nc % dseq == 0 and dseq % seq == 0 and dseq % GRID_W == 0

    tm = _pick_tile(512, nc, dseq)
    tm_ab = _pick_tile(256, nc, dseq)
    tm_c = _pick_tile(512, nc, dseq)
    tm_f = _pick_tile(1024, nc, dseq)
    tf = _pick_tile(512, dff)
    assert tm % seq == 0

    x, x2, second = x_prompt.reshape(nc, d), x_sample.reshape(ns, d), "select"

    groups = 1 + dbatch
    gp = -(-groups // SUBLANES) * SUBLANES
    cond = jnp.concatenate([c_ctx[None, :], c, jnp.zeros((gp - groups, d), F32)], axis=0)
    mods_all = _adaln(cond, w_ada, b_ada).reshape(depth, gp, 6, d)

    rope_c, rope_s = _rope_tables(dseq, tm_c)
    ident_c = jnp.concatenate([jnp.ones((past, ROPE), F32), jnp.zeros((past, LANES - ROPE), F32)], axis=1)
    shape_kw = dict(nc=nc, dseq=dseq)

    w_in_ab_b = w_in_ab.astype(BF16)
    w_out_ab_b = w_out_ab.astype(BF16)
    w_o_c_b = w_o_c.astype(BF16)
    w_ffn_b = (w_ffn_in[0].astype(BF16), w_ffn_out[0].astype(BF16))

    nat = None
    mla_ckv, mla_kr = [], []
    for l in range(depth):
        i = l // 2
        mods = mods_all[l]
        if l % 2 == 0:
            gq = (g_qn_a[i] * (DH_A ** -0.5 * LOG2E))[None, :]
            qkv, gb, z, nk, nv = _inproj_ab(
                x, x2, second, mods, norm1_g[l][None, :], w_in_ab_b, gq, g_kn_a[i][None, :], nat, i,
                seq=seq, batch=batch, n_even=n_even, tm=tm_ab, **shape_kw)
            nat = (nk, nv)
            ya_c = _ctx_attn(qkv, qkv, qkv, (0, 1, 2), nc=nc, batch=batch, seq=seq,
                             heads=H_A, dq=DH_A, dv=DH_A)
            ya_s = _nat_attn(rpb_a[i], qkv, cache_nat_k, cache_nat_v, i, dbatch=dbatch, **shape_kw)
            x, h2 = _outproj(x, x2, second, mods, norm2_g[l][None, :], ya_c, ya_s, w_out_ab_b, i,
                             (gb, z, conv_b_w[i]), seq=seq, tm=tm, **shape_kw)
        else:
            wd = jnp.pad(w_down_c[i], ((0, 0), (0, LANES - ROPE))).astype(BF16)
            wq = jnp.pad(w_uq_c[i].reshape(Q_LORA, H_C, QK_DIM),
                         ((0, 0), (0, 0), (0, QK_PAD - QK_DIM))).reshape(Q_LORA, H_C * QK_PAD).astype(BF16)
            wkv = w_ukv_c[i].astype(BF16)
            gq = jnp.pad(g_qn_c[i] * (QK_DIM ** -0.5 * LOG2E), (0, QK_PAD - QK_DIM))[None, :]
            gkn = g_kn_c[i][None, :NOPE]
            gkr = jnp.pad(g_kn_c[i][NOPE:], (0, LANES - ROPE))[None, :]
            q, k, v, ckv_n, kr = _inproj_c(
                x, x2, second, mods, norm1_g[l][None, :], wd, g_cq[i][None, :], g_ckv[i][None, :],
                wq, gq, wkv, gkn, gkr, rope_c, rope_s, tm=tm_c, **shape_kw)
            mla_ckv.append(ckv_n.reshape(batch, seq, KV_LORA))
            mla_kr.append(kr.reshape(batch, seq, ROPE))
            kx, vx = _cache_kv(
                cache_mla_ckv[:, i].reshape(dbatch * past, KV_LORA),
                jnp.pad(cache_mla_krope[:, i].reshape(dbatch * past, ROPE), ((0, 0), (0, LANES - ROPE))),
                wkv, gkn, gkr, ident_c, jnp.zeros((past, LANES), F32), tm=past)
            ya_c = _ctx_attn(q, k, v, (0, 0, 0), nc=nc, batch=batch, seq=seq,
                             heads=H_C, dq=QK_PAD, dv=V_DIM)
            ya_s = _mla_attn(q, k, v, kx, vx, dbatch=dbatch, past=past,
                             tq=_pick_tile(512, dseq), **shape_kw)
            x, h2 = _outproj(x, x2, second, mods, norm2_g[l][None, :], ya_c, ya_s, w_o_c_b, i,
                             seq=seq, tm=tm, **shape_kw)
        last = l == depth - 1
        res = _ffn(x, h2, mods, w_ffn_b[0], w_ffn_b[1], None if last else (w_ffn_in, w_ffn_out), l,
                   last=last, tm=tm if last else tm_f, tf=tf, **shape_kw)
        if last:
            x2 = res
        else:
            x2, second, w_ffn_b = res[0], "add", res[1:]

    return (x2[0].reshape(batch, seq, d), x2[1].reshape(dbatch, dseq, d), nat[0], nat[1],
            jnp.stack(mla_ckv, axis=1), jnp.stack(mla_kr, axis=1))
```

```python
import functools

import numpy as np
import jax
import jax.numpy as jnp
from jax import lax
from jax.experimental import pallas as pl
from jax.experimental.pallas import tpu as pltpu

F32 = jnp.float32
BF16 = jnp.bfloat16

EPS = 1e-6
GRID_W = 64
WIN_R = 8
WIN_C = 16
ROPE_BASE = 10000.0
H_A = 8
DH_A = 128
H_C = 16
Q_LORA = 512
KV_LORA = 256
NOPE = 128
ROPE = 64
V_DIM = 128
QK_DIM = NOPE + ROPE
QK_PAD = 256
LANES = 128
SUBLANES = 8
HALO = 16
NEG = -1e30
LOG2E = 1.4426950408889634

VMEM_LIMIT = 56 * 1024 * 1024
ROW_BLOCK = 128
ROW_BLOCK_AB = 256
ROW_BLOCK_C = 256
VMEM_LIMIT_C = 61 * 1024 * 1024


def _params(*sem, vmem=VMEM_LIMIT):
    return pltpu.CompilerParams(dimension_semantics=sem, vmem_limit_bytes=vmem)


def _group(i, nct, per):
    return jnp.where(i < nct, 0, 1 + (i - nct) // per)


def _dot(a, b):
    return jnp.dot(a, b, preferred_element_type=F32)


def _dot_nt(a, b):
    return lax.dot_general(a, b, (((1,), (1,)), ((), ())), preferred_element_type=F32)


def _ada_kernel(c_ref, w_ref, b_ref, o_ref):
    c = c_ref[...]
    s = (c / (1.0 + jnp.exp(-c))).astype(BF16)
    o_ref[...] = _dot(s, w_ref[...].astype(BF16)) + b_ref[...]


def _adaln(cond, w_ada, b_ada):
    depth, d, n6 = w_ada.shape
    gp = cond.shape[0]
    tn = 1024 if n6 % 1024 == 0 else n6
    return pl.pallas_call(
        _ada_kernel,
        out_shape=jax.ShapeDtypeStruct((depth, gp, n6), F32),
        grid=(depth, n6 // tn),
        in_specs=[
            pl.BlockSpec((gp, d), lambda l, n: (0, 0)),
            pl.BlockSpec((None, d, tn), lambda l, n: (l, 0, n)),
            pl.BlockSpec((None, 1, tn), lambda l, n: (l, 0, n)),
        ],
        out_specs=pl.BlockSpec((None, gp, tn), lambda l, n: (l, 0, n)),
        compiler_params=_params("arbitrary", "arbitrary"),
        name="adaln",
    )(cond, w_ada, b_ada.reshape(depth, 1, n6))


def _row_spec(tm, d, nct, second, which):
    if second != "select":
        return pl.BlockSpec((tm, d), lambda i, *_: (i, 0))
    if which == 0:
        return pl.BlockSpec((tm, d), lambda i, *_: (jnp.minimum(i, nct - 1), 0))
    return pl.BlockSpec((tm, d), lambda i, *_: (jnp.maximum(i - nct, 0), 0))


def _residual(x_ref, x2_ref, rows, second, is_ctx):
    x = x_ref[rows, :]
    if second == "add":
        x = x + x2_ref[rows, :]
    elif second == "select":
        x = jnp.where(is_ctx, x, x2_ref[rows, :])
    return x


def _inproj_ab_kernel(x_ref, *rest, nct, seq, first, rb, second):
    x2_ref = None
    if second:
        x2_ref, *rest = rest
    mod_ref, g1_ref, w_ref, gq_ref, gk_ref, *rest = rest
    if first:
        qkv_ref, gb_ref, z_ref, natk_ref, natv_ref = rest
    else:
        _, _, qkv_ref, gb_ref, z_ref, natk_ref, natv_ref = rest
    i = pl.program_id(0)
    tm = x_ref.shape[0]
    na = H_A * DH_A
    is_ctx = i < nct
    gs = g1_ref[...] * (1.0 + mod_ref[1:2, :])

    def nat_store(nat_ref, rows, hh, val):
        bb, off = rows.start // seq, rows.start % seq
        if first:
            nat_ref[bb, 0, hh, off:off + rb, :] = val
        else:
            nat_ref[bb, hh, off:off + rb, :] = val

    def head_norm(rows, r, col, gain_ref, nat_ref):
        for hh in range(H_A):
            t = r[:, hh * DH_A:(hh + 1) * DH_A]
            ms = jnp.mean(t * t, axis=-1, keepdims=True)
            val = t * lax.rsqrt(ms + EPS) * gain_ref[...]
            qkv_ref[rows, col * na + hh * DH_A:col * na + (hh + 1) * DH_A] = val.astype(BF16)
            if nat_ref is not None:
                nat_store(nat_ref, rows, hh, val)

    def tile(ctx):
        if ctx and first and natk_ref.shape[1] > 1:
            natk_ref[:, 1:] = jnp.zeros_like(natk_ref[:, 1:])
            natv_ref[:, 1:] = jnp.zeros_like(natv_ref[:, 1:])
        for r0 in range(0, tm, rb):
            rows = slice(r0, r0 + rb)
            x = _residual(x_ref, x2_ref, rows, second, is_ctx)
            ms = jnp.mean(x * x, axis=-1, keepdims=True)
            h = (x * lax.rsqrt(ms + EPS) * gs + mod_ref[0:1, :]).astype(BF16)

            def proj(col):
                return _dot(h, w_ref[:, col * na:(col + 1) * na])

            head_norm(rows, proj(0), 0, gq_ref, None)
            head_norm(rows, proj(1), 1, gk_ref, natk_ref if ctx else None)
            v = proj(2)
            qkv_ref[rows, 2 * na:3 * na] = v.astype(BF16)
            if ctx:
                for hh in range(H_A):
                    nat_store(natv_ref, rows, hh, v[:, hh * DH_A:(hh + 1) * DH_A])
            gb_ref[rows, :] = proj(3).astype(BF16)
            gc = proj(4)
            z_ref[rows, :] = (gc * proj(5)).astype(BF16)

    pl.when(is_ctx)(lambda: tile(True))
    pl.when(jnp.logical_not(is_ctx))(lambda: tile(False))


def _inproj_ab(x, x2, second, mods, g1, w_in, gq, gk, nat_prev, slot,
               *, nc, seq, dseq, batch, n_even, tm):
    d = x.shape[1]
    n = nc + (x2.shape[0] if second == "select" else x.shape[0] - nc)
    na = H_A * DH_A
    nct = nc // tm
    per = dseq // tm
    first = nat_prev is None
    kern = functools.partial(_inproj_ab_kernel, nct=nct, seq=seq, first=first,
                             rb=_pick_tile(ROW_BLOCK_AB, tm, seq), second=second)
    nat_shape = jax.ShapeDtypeStruct((batch, n_even, H_A, seq, DH_A), F32)
    if first:
        nat_spec = pl.BlockSpec((tm // seq, n_even, H_A, seq, DH_A),
                                lambda i: (jnp.minimum(i, nct - 1), 0, 0, 0, 0))
    else:
        nat_spec = pl.BlockSpec((tm // seq, None, H_A, seq, DH_A),
                                lambda i: (jnp.minimum(i, nct - 1), slot, 0, 0, 0))
    in_specs = [_row_spec(tm, d, nct, second, 0)] + ([_row_spec(tm, d, nct, second, 1)] if second else []) + [
        pl.BlockSpec((None, 6, d), lambda i: (_group(i, nct, per), 0, 0)),
        pl.BlockSpec((1, d), lambda i: (0, 0)),
        pl.BlockSpec((None,) + w_in.shape[1:], lambda i: (slot, 0, 0), pipeline_mode=pl.Buffered(1)),
        pl.BlockSpec((1, DH_A), lambda i: (0, 0)),
        pl.BlockSpec((1, DH_A), lambda i: (0, 0)),
    ]
    args = [x] + ([x2] if second else []) + [mods, g1, w_in, gq, gk]
    aliases = {}
    if not first:
        in_specs += [pl.BlockSpec(memory_space=pl.ANY)] * 2
        aliases = {len(args): 3, len(args) + 1: 4}
        args += list(nat_prev)
    return pl.pallas_call(
        kern,
        out_shape=(jax.ShapeDtypeStruct((n, 3 * na), BF16),
                   jax.ShapeDtypeStruct((n, na), BF16),
                   jax.ShapeDtypeStruct((n, na), BF16),
                   nat_shape, nat_shape),
        grid=(n // tm,),
        in_specs=in_specs,
        out_specs=(
            pl.BlockSpec((tm, 3 * na), lambda i: (i, 0)),
            pl.BlockSpec((tm, na), lambda i: (i, 0)),
            pl.BlockSpec((tm, na), lambda i: (i, 0)),
            nat_spec, nat_spec,
        ),
        input_output_aliases=aliases,
        compiler_params=_params("arbitrary"),
        name="inproj_ab",
    )(*args)


def _ctx_attn_kernel(q_ref, k_ref, v_ref, o_ref, *, heads, dq, dv):
    def scores(hh):
        return _dot_nt(q_ref[:, hh * dq:(hh + 1) * dq], k_ref[:, hh * dq:(hh + 1) * dq])

    s_next = scores(0)
    for hh in range(heads):
        s = s_next
        if hh + 1 < heads:
            s_next = scores(hh + 1)
        m = jnp.max(s, axis=-1, keepdims=True)
        p = jnp.exp2(s - m)
        l = jnp.sum(p, axis=-1, keepdims=True)
        o = _dot(p.astype(BF16), v_ref[:, hh * dv:(hh + 1) * dv]) / l
        o_ref[:, hh * dv:(hh + 1) * dv] = o.astype(BF16)


def _ctx_attn(q_arr, k_arr, v_arr, cols, *, nc, batch, seq, heads, dq, dv):
    cq, ck, cv = cols
    kern = functools.partial(_ctx_attn_kernel, heads=heads, dq=dq, dv=dv)
    return pl.pallas_call(
        kern,
        out_shape=jax.ShapeDtypeStruct((nc, heads * dv), BF16),
        grid=(batch,),
        in_specs=[
            pl.BlockSpec((seq, heads * dq), lambda b: (b, cq)),
            pl.BlockSpec((seq, heads * dq), lambda b: (b, ck)),
            pl.BlockSpec((seq, heads * dv), lambda b: (b, cv)),
        ],
        out_specs=pl.BlockSpec((seq, heads * dv), lambda b: (b, 0)),
        compiler_params=_params("arbitrary"),
        name="ctx_attn",
    )(q_arr, k_arr, v_arr)


NAT_QR = 8
NAT_KR = 16


def _nat_window(a, rows):
    return int(np.clip(a * NAT_QR - WIN_R // 2, 0, rows - NAT_KR))


def _nat_attn_kernel(rpb_ref, q_ref, k_ref, v_ref, ck_ref, cv_ref, o_ref,
                     t2_ref, bias_ref, *, rows):
    hh = pl.program_id(0)
    b = pl.program_id(1)
    nblk = rows // NAT_QR
    n_dr = 2 * WIN_R - 1
    n_dc = 2 * WIN_C - 1

    @pl.when(b == 0)
    def _():
        j = lax.broadcasted_iota(jnp.int32, (GRID_W, LANES), 0)
        kc = lax.broadcasted_iota(jnp.int32, (GRID_W, LANES), 1) % GRID_W
        cs = jnp.clip(j - WIN_C // 2, 0, GRID_W - WIN_C)
        col_in = (kc >= cs) & (kc < cs + WIN_C)
        dc = kc - j + (WIN_C - 1)
        for d in range(n_dr):
            t = jnp.full((GRID_W, LANES), NEG, F32)
            for c in range(n_dc):
                t = jnp.where(col_in & (dc == c), rpb_ref[hh, d * n_dc + c] * LOG2E, t)
            t2_ref[d] = t
        lane = lax.broadcasted_iota(jnp.int32, (GRID_W, LANES), 1)
        negt = jnp.full((GRID_W, LANES), NEG, F32)
        for a in range(nblk):
            ws = _nat_window(a, rows)
            for qr in range(NAT_QR):
                r = a * NAT_QR + qr
                rs = int(np.clip(r - WIN_R // 2, 0, rows - WIN_R))
                for pp in range(NAT_KR // 2):
                    halves = []
                    for kk in (2 * pp, 2 * pp + 1):
                        kr = ws + kk
                        halves.append(t2_ref[kr - r + WIN_R - 1] if rs <= kr < rs + WIN_R else None)
                    if halves[0] is None and halves[1] is None:
                        blk = negt
                    else:
                        left = negt if halves[0] is None else halves[0]
                        right = negt if halves[1] is None else halves[1]
                        blk = jnp.where(lane < GRID_W, left, right)
                    bias_ref[a, qr * GRID_W:(qr + 1) * GRID_W, pp * LANES:(pp + 1) * LANES] = blk

    ck = ck_ref[...].astype(BF16)
    cv = cv_ref[...].astype(BF16)
    tq = NAT_QR * GRID_W
    tk = NAT_KR * GRID_W

    def scores(a):
        ws = _nat_window(a, rows)
        q = q_ref[a * tq:(a + 1) * tq, :]
        kw = k_ref[ws * GRID_W:ws * GRID_W + tk, :]
        return _dot_nt(q, kw) + bias_ref[a], _dot_nt(q, ck)

    s_next = scores(0)
    for a in range(nblk):
        s_loc, s_ctx = s_next
        if a + 1 < nblk:
            s_next = scores(a + 1)
        ws = _nat_window(a, rows)
        vw = v_ref[ws * GRID_W:ws * GRID_W + tk, :]
        m = jnp.maximum(jnp.max(s_loc, axis=-1, keepdims=True),
                        jnp.max(s_ctx, axis=-1, keepdims=True))
        p_loc = jnp.exp2(s_loc - m)
        p_ctx = jnp.exp2(s_ctx - m)
        l = jnp.sum(p_loc, axis=-1, keepdims=True) + jnp.sum(p_ctx, axis=-1, keepdims=True)
        o = (_dot(p_loc.astype(BF16), vw) + _dot(p_ctx.astype(BF16), cv)) / l
        o_ref[a * tq:(a + 1) * tq, :] = o.astype(BF16)


def _nat_attn(rpb, qkv, cache_k, cache_v, slot, *, nc, dseq, dbatch):
    rows = dseq // GRID_W
    assert rows % NAT_QR == 0 and rows >= NAT_KR
    nblk = rows // NAT_QR
    past = cache_k.shape[3]
    off = nc // dseq
    n_dr, n_dc = 2 * WIN_R - 1, 2 * WIN_C - 1
    kern = functools.partial(_nat_attn_kernel, rows=rows)
    cache_spec = pl.BlockSpec((None, None, None, past, DH_A), lambda h, b: (b, slot, h, 0, 0))
    return pl.pallas_call(
        kern,
        out_shape=jax.ShapeDtypeStruct((dbatch * dseq, H_A * DH_A), BF16),
        grid=(H_A, dbatch),
        in_specs=[
            pl.BlockSpec(memory_space=pltpu.SMEM),
            pl.BlockSpec((dseq, DH_A), lambda h, b: (off + b, h)),
            pl.BlockSpec((dseq, DH_A), lambda h, b: (off + b, H_A + h)),
            pl.BlockSpec((dseq, DH_A), lambda h, b: (off + b, 2 * H_A + h)),
            cache_spec, cache_spec,
        ],
        out_specs=pl.BlockSpec((dseq, DH_A), lambda h, b: (b, h)),
        scratch_shapes=[pltpu.VMEM((n_dr, GRID_W, LANES), F32),
                        pltpu.VMEM((nblk, NAT_QR * GRID_W, NAT_KR * GRID_W), F32)],
        compiler_params=_params("arbitrary", "arbitrary"),
        name="nat_attn",
    )(rpb.reshape(H_A, n_dr * n_dc), qkv, qkv, qkv, cache_k, cache_v)


def _outproj_kernel(x_ref, *refs, conv, nct, seq, dseq, rb, second):
    x2_ref = None
    if second:
        x2_ref, *refs = refs
    if conv:
        (mod_ref, g2_ref, yac_ref, yas_ref, gb_ref, z_ref, zp_ref, zn_ref, cw_ref, w_ref,
         o_ref, h2_ref, zs_ref) = refs
    else:
        mod_ref, g2_ref, yac_ref, yas_ref, w_ref, o_ref, h2_ref = refs
    i = pl.program_id(0)
    tm = x_ref.shape[0]
    ka = yac_ref.shape[1]
    if conv:
        zs_ref[0:HALO, :] = zp_ref[...].astype(F32)
        zs_ref[HALO:HALO + tm, :] = z_ref[...].astype(F32)
        zs_ref[HALO + tm:2 * HALO + tm, :] = zn_ref[...].astype(F32)
        last = jnp.where(i < nct, seq - 1, dseq - 1)
    gate = mod_ref[2:3, :]
    gs = g2_ref[...] * (1.0 + mod_ref[4:5, :])
    shift = mod_ref[3:4, :]
    for r0 in range(0, tm, rb):
        rows = slice(r0, r0 + rb)
        ya = jnp.where(i < nct, yac_ref[rows, :], yas_ref[rows, :])
        y = _dot(ya, w_ref[0:ka, :])
        if conv:
            row = r0 + lax.broadcasted_iota(jnp.int32, (rb, 1), 0)
            pos = jnp.where(i < nct, row % seq, ((i - nct) * tm + row) % dseq)
            z_prev = jnp.where(pos == 0, 0.0, zs_ref[HALO - 1 + r0:HALO - 1 + r0 + rb, :])
            z_next = jnp.where(pos == last, 0.0, zs_ref[HALO + 1 + r0:HALO + 1 + r0 + rb, :])
            z_mid = zs_ref[HALO + r0:HALO + r0 + rb, :]
            cv = cw_ref[0:1, :] * z_prev + cw_ref[1:2, :] * z_mid + cw_ref[2:3, :] * z_next
            yb = gb_ref[rows, :].astype(F32) * cv
            y = y + _dot(yb.astype(BF16), w_ref[ka:, :])
        xn = _residual(x_ref, x2_ref, rows, second, i < nct) + gate * y
        o_ref[rows, :] = xn
        ms = jnp.mean(xn * xn, axis=-1, keepdims=True)
        h2_ref[rows, :] = (xn * lax.rsqrt(ms + EPS) * gs + shift).astype(BF16)


def _outproj(x, x2, second, mods, g2, ya_c, ya_s, w_out, slot, conv_in=None, *, nc, seq, dseq, tm):
    d = x.shape[1]
    n = nc + ya_s.shape[0]
    ka = ya_c.shape[1]
    nct = nc // tm
    per = dseq // tm
    conv = conv_in is not None
    kern = functools.partial(_outproj_kernel, conv=conv, nct=nct, seq=seq, dseq=dseq,
                             rb=_pick_tile(ROW_BLOCK, tm), second=second)
    in_specs = [_row_spec(tm, d, nct, second, 0)] + ([_row_spec(tm, d, nct, second, 1)] if second else []) + [
        pl.BlockSpec((None, 6, d), lambda i: (_group(i, nct, per), 0, 0)),
        pl.BlockSpec((1, d), lambda i: (0, 0)),
        pl.BlockSpec((tm, ka), lambda i: (jnp.minimum(i, nct - 1), 0)),
        pl.BlockSpec((tm, ka), lambda i: (jnp.maximum(i - nct, 0), 0)),
    ]
    args = [x] + ([x2] if second else []) + [mods, g2, ya_c, ya_s]
    scratch = []
    if conv:
        gb, z, cw = conv_in
        nb = z.shape[1]
        th = tm // HALO
        nh = n // HALO
        in_specs += [
            pl.BlockSpec((tm, nb), lambda i: (i, 0)),
            pl.BlockSpec((tm, nb), lambda i: (i, 0)),
            pl.BlockSpec((HALO, nb), lambda i: (jnp.maximum(i * th - 1, 0), 0)),
            pl.BlockSpec((HALO, nb), lambda i: (jnp.minimum((i + 1) * th, nh - 1), 0)),
            pl.BlockSpec((3, nb), lambda i: (0, 0)),
        ]
        args += [gb, z, z, z, cw]
        scratch = [pltpu.VMEM((tm + 2 * HALO, nb), F32)]
    in_specs.append(pl.BlockSpec((None,) + w_out.shape[1:], lambda i: (slot, 0, 0),
                                 pipeline_mode=pl.Buffered(1)))
    args.append(w_out)
    return pl.pallas_call(
        kern,
        out_shape=(jax.ShapeDtypeStruct((n, d), F32), jax.ShapeDtypeStruct((n, d), BF16)),
        grid=(n // tm,),
        in_specs=in_specs,
        out_specs=(pl.BlockSpec((tm, d), lambda i: (i, 0)), pl.BlockSpec((tm, d), lambda i: (i, 0))),
        scratch_shapes=scratch,
        compiler_params=_params("arbitrary"),
        name="outproj_conv" if conv else "outproj",
    )(*args)


def _ffn_kernel(*refs, last, nct, cast_next):
    if last:
        x_ref, h_ref, mod_ref, wg_ref, wu_ref, wo_ref, oc_ref, os_ref, acc_ref = refs
    elif cast_next:
        (h_ref, mod_ref, wg_ref, wu_ref, wo_ref, nin_ref, nout_ref,
         acc_ref, nin_b_ref, nout_b_ref) = refs
    else:
        h_ref, mod_ref, wg_ref, wu_ref, wo_ref, acc_ref = refs
    i = pl.program_id(0)
    f = pl.program_id(1)

    def step(start):
        h = h_ref[...]
        g = _dot(h, wg_ref[...])
        u = _dot(h, wu_ref[...])
        a = (g / (1.0 + jnp.exp(-g))) * u
        part = _dot(a.astype(BF16), wo_ref[...])
        acc_ref[...] = part if start else acc_ref[...] + part
        if cast_next:
            nin_b_ref[...] = nin_ref[...].astype(BF16)
            nout_b_ref[...] = nout_ref[...].astype(BF16)

    pl.when(f == 0)(lambda: step(True))
    pl.when(f != 0)(lambda: step(False))

    @pl.when(f == pl.num_programs(1) - 1)
    def _():
        delta = mod_ref[5:6, :] * acc_ref[...]
        if last:
            @pl.when(i < nct)
            def _():
                oc_ref[...] = x_ref[...] + delta

            @pl.when(i >= nct)
            def _():
                os_ref[...] = x_ref[...] + delta
        else:
            acc_ref[...] = delta


def _slab_rows(total, steps):
    r = HALO
    while total % r or total // r > steps:
        r += HALO
    return r


def _ffn(x, h2, mods, w_in, w_out, next_w, layer, *, last, nc, dseq, tm, tf):
    n, d = h2.shape
    dff = w_out.shape[0]
    nf = dff // tf
    nct = nc // tm
    per = dseq // tm
    cast_next = next_w is not None
    kern = functools.partial(_ffn_kernel, last=last, nct=nct, cast_next=cast_next)
    row_spec = pl.BlockSpec((tm, d), lambda i, f: (i, 0))
    in_specs = [
        row_spec,
        pl.BlockSpec((None, 6, d), lambda i, f: (_group(i, nct, per), 0, 0)),
        pl.BlockSpec((d, tf), lambda i, f: (0, f)),
        pl.BlockSpec((d, tf), lambda i, f: (0, nf + f)),
        pl.BlockSpec((tf, d), lambda i, f: (f, 0)),
    ]
    args = [h2, mods, w_in, w_in, w_out]
    scratch = []
    if last:
        in_specs.insert(0, row_spec)
        args.insert(0, x)
        out_shape = (jax.ShapeDtypeStruct((nc, d), F32), jax.ShapeDtypeStruct((n - nc, d), F32))
        out_specs = (pl.BlockSpec((tm, d), lambda i, f: (jnp.minimum(i, nct - 1), 0)),
                     pl.BlockSpec((tm, d), lambda i, f: (jnp.maximum(i - nct, 0), 0)))
        scratch = [pltpu.VMEM((tm, d), F32)]
    else:
        out_shape = jax.ShapeDtypeStruct((n, d), F32)
        out_specs = row_spec
    if cast_next:
        steps = (n // tm) * nf
        r_in, r_out = _slab_rows(d, steps), _slab_rows(dff, steps)
        slab = lambda nslab: (lambda i, f: (jnp.minimum(i * nf + f, nslab - 1), 0))
        slab3 = lambda nslab: (lambda i, f: (layer + 1, jnp.minimum(i * nf + f, nslab - 1), 0))
        in_specs += [pl.BlockSpec((None, r_in, 2 * dff), slab3(d // r_in)),
                     pl.BlockSpec((None, r_out, d), slab3(dff // r_out))]
        args += list(next_w)
        out_shape = (out_shape, jax.ShapeDtypeStruct((d, 2 * dff), BF16),
                     jax.ShapeDtypeStruct((dff, d), BF16))
        out_specs = (out_specs, pl.BlockSpec((r_in, 2 * dff), slab(d // r_in)),
                     pl.BlockSpec((r_out, d), slab(dff // r_out)))
    return pl.pallas_call(
        kern,
        out_shape=out_shape,
        grid=(n // tm, nf),
        in_specs=in_specs,
        out_specs=out_specs,
        scratch_shapes=scratch,
        compiler_params=_params("arbitrary", "arbitrary"),
        name="ffn",
    )(*args)


def _rope_low_half(rows):
    lane = lax.broadcasted_iota(jnp.int32, (rows, LANES), 1)
    return lane % (ROPE // 2) < ROPE // 4


def _rope(x, c, s, low):
    quarter = ROPE // 4
    swapped = jnp.where(low, pltpu.roll(x, LANES - quarter, 1), pltpu.roll(x, quarter, 1))
    return x * c + swapped * s


def _mla_kv_expand(ckv_bf16, krp, wkv_ref, gkn_ref, gkr_ref, c, s, low, k_ref, v_ref, rows=slice(None)):
    kv = _dot(ckv_bf16, wkv_ref[...])
    krr = _rope(krp * gkr_ref[...], c, s, low)
    ss_kr = jnp.sum(krp * krp, axis=-1, keepdims=True)
    for hh in range(H_C):
        base = hh * (NOPE + V_DIM)
        kn = kv[:, base:base + NOPE]
        rstd = lax.rsqrt((jnp.sum(kn * kn, axis=-1, keepdims=True) + ss_kr) / QK_DIM + EPS)
        k_ref[rows, hh * QK_PAD:hh * QK_PAD + NOPE] = (kn * rstd * gkn_ref[...]).astype(BF16)
        k_ref[rows, hh * QK_PAD + NOPE:(hh + 1) * QK_PAD] = (krr * rstd).astype(BF16)
        v_ref[rows, hh * V_DIM:(hh + 1) * V_DIM] = kv[:, base + NOPE:base + NOPE + V_DIM].astype(BF16)


def _inproj_c_kernel(x_ref, *rest, nct, rb, second):
    x2_ref = None
    if second:
        x2_ref, *rest = rest
    (mod_ref, g1_ref, wd_ref, gcq_ref, gckv_ref, wq_ref, gq_ref, wkv_ref, gkn_ref, gkr_ref,
     c_ref, s_ref, q_ref, k_ref, v_ref, ckv_ref, kr_ref) = rest
    i = pl.program_id(0)
    tm = x_ref.shape[0]
    blocks = [slice(r * rb, (r + 1) * rb) for r in range(tm // rb)]
    gs = g1_ref[...] * (1.0 + mod_ref[1:2, :])
    low = _rope_low_half(rb)

    def latents(rows):
        x = _residual(x_ref, x2_ref, rows, second, i < nct)
        ms = jnp.mean(x * x, axis=-1, keepdims=True)
        h = (x * lax.rsqrt(ms + EPS) * gs + mod_ref[0:1, :]).astype(BF16)
        dn = _dot(h, wd_ref[...])
        cq = dn[:, :Q_LORA]
        ckv = dn[:, Q_LORA:Q_LORA + KV_LORA]
        krp = dn[:, Q_LORA + KV_LORA:]
        cqn = cq * lax.rsqrt(jnp.mean(cq * cq, axis=-1, keepdims=True) + EPS) * gcq_ref[...]
        ckvn = ckv * lax.rsqrt(jnp.mean(ckv * ckv, axis=-1, keepdims=True) + EPS) * gckv_ref[...]
        return cqn.astype(BF16), ckvn, krp

    def expand(rows, cqn, ckvn, krp, ctx):
        if ctx:
            ckv_ref[rows, :] = ckvn
            kr_ref[rows, :] = krp[:, :ROPE]
        c = c_ref[rows, :]
        s = s_ref[rows, :]
        q = _dot(cqn, wq_ref[...])
        for hh in range(H_C):
            qh = q[:, hh * QK_PAD:(hh + 1) * QK_PAD]
            rstd = lax.rsqrt(jnp.sum(qh * qh, axis=-1, keepdims=True) / QK_DIM + EPS)
            qn = qh * rstd * gq_ref[...]
            q_ref[rows, hh * QK_PAD:hh * QK_PAD + NOPE] = qn[:, :NOPE].astype(BF16)
            q_ref[rows, hh * QK_PAD + NOPE:(hh + 1) * QK_PAD] = _rope(qn[:, NOPE:], c, s, low).astype(BF16)
        _mla_kv_expand(ckvn.astype(BF16), krp, wkv_ref, gkn_ref, gkr_ref, c, s, low, k_ref, v_ref, rows)

    def tile(ctx):
        nxt = latents(blocks[0])
        for r, rows in enumerate(blocks):
            cur = nxt
            if r + 1 < len(blocks):
                nxt = latents(blocks[r + 1])
            expand(rows, *cur, ctx)

    pl.when(i < nct)(lambda: tile(True))
    pl.when(i >= nct)(lambda: tile(False))


def _inproj_c(x, x2, second, mods, g1, wd, gcq, gckv, wq, gq, wkv, gkn, gkr, rope_c, rope_s,
              *, nc, dseq, tm):
    d = x.shape[1]
    n = nc + (x2.shape[0] if second == "select" else x.shape[0] - nc)
    nct = nc // tm
    per = dseq // tm
    kern = functools.partial(_inproj_c_kernel, nct=nct, rb=_pick_tile(ROW_BLOCK_C, tm), second=second)
    const = lambda shape: pl.BlockSpec(shape, lambda i: (0,) * len(shape),
                                       pipeline_mode=pl.Buffered(1))
    rope_idx = lambda i: (jnp.where(i < nct, 0, 1 + (i - nct) % per), 0)
    ctx_idx = lambda i: (jnp.minimum(i, nct - 1), 0)
    return pl.pallas_call(
        kern,
        out_shape=(jax.ShapeDtypeStruct((n, H_C * QK_PAD), BF16),
                   jax.ShapeDtypeStruct((n, H_C * QK_PAD), BF16),
                   jax.ShapeDtypeStruct((n, H_C * V_DIM), BF16),
                   jax.ShapeDtypeStruct((nc, KV_LORA), F32),
                   jax.ShapeDtypeStruct((nc, ROPE), F32)),
        grid=(n // tm,),
        in_specs=[_row_spec(tm, d, nct, second, 0)] + ([_row_spec(tm, d, nct, second, 1)] if second else []) + [
            pl.BlockSpec((None, 6, d), lambda i: (_group(i, nct, per), 0, 0)),
            const((1, d)), const(wd.shape), const((1, Q_LORA)), const((1, KV_LORA)),
            const(wq.shape), const((1, QK_PAD)), const(wkv.shape),
            const((1, NOPE)), const((1, LANES)),
            pl.BlockSpec((tm, LANES), rope_idx),
            pl.BlockSpec((tm, LANES), rope_idx),
        ],
        out_specs=(
            pl.BlockSpec((tm, H_C * QK_PAD), lambda i: (i, 0)),
            pl.BlockSpec((tm, H_C * QK_PAD), lambda i: (i, 0)),
            pl.BlockSpec((tm, H_C * V_DIM), lambda i: (i, 0)),
            pl.BlockSpec((tm, KV_LORA), ctx_idx),
            pl.BlockSpec((tm, ROPE), ctx_idx),
        ),
        compiler_params=_params("arbitrary", vmem=VMEM_LIMIT_C),
        name="inproj_c",
    )(*([x] + ([x2] if second else []) + [mods, g1, wd, gcq, gckv, wq, gq, wkv, gkn, gkr, rope_c, rope_s]))


def _cache_kv_kernel(ckv_ref, krp_ref, wkv_ref, gkn_ref, gkr_ref, c_ref, s_ref, k_ref, v_ref):
    _mla_kv_expand(ckv_ref[...].astype(BF16), krp_ref[...], wkv_ref, gkn_ref, gkr_ref,
                   c_ref[...], s_ref[...], _rope_low_half(ckv_ref.shape[0]), k_ref, v_ref)


def _cache_kv(ckv, krp, wkv, gkn, gkr, rope_c, rope_s, *, tm):
    n = ckv.shape[0]
    const = lambda shape: pl.BlockSpec(shape, lambda i: (0,) * len(shape))
    return pl.pallas_call(
        _cache_kv_kernel,
        out_shape=(jax.ShapeDtypeStruct((n, H_C * QK_PAD), BF16),
                   jax.ShapeDtypeStruct((n, H_C * V_DIM), BF16)),
        grid=(n // tm,),
        in_specs=[
            pl.BlockSpec((tm, KV_LORA), lambda i: (i, 0)),
            pl.BlockSpec((tm, LANES), lambda i: (i, 0)),
            const(wkv.shape), const((1, NOPE)), const((1, LANES)),
            const((tm, LANES)), const((tm, LANES)),
        ],
        out_specs=(pl.BlockSpec((tm, H_C * QK_PAD), lambda i: (i, 0)),
                   pl.BlockSpec((tm, H_C * V_DIM), lambda i: (i, 0))),
        compiler_params=_params("arbitrary"),
        name="cache_kv",
    )(ckv, krp, wkv, gkn, gkr, rope_c, rope_s)


def _mla_attn_kernel(q_ref, k_ref, v_ref, kx_ref, vx_ref, o_ref, kall_ref, vt_ref, *, tq):
    past = kx_ref.shape[0]
    kall_ref[0:past, :] = kx_ref[...]
    kall_ref[past:, :] = k_ref[...]
    vt_ref[:, 0:past] = vx_ref[...].astype(F32).T.astype(BF16)
    vt_ref[:, past:] = v_ref[...].astype(F32).T.astype(BF16)

    def scores(t):
        q = q_ref[t * tq:(t + 1) * tq, :]
        return _dot_nt(kall_ref[...], q)

    nblk = q_ref.shape[0] // tq
    s_next = scores(0)
    for t in range(nblk):
        s = s_next
        if t + 1 < nblk:
            s_next = scores(t + 1)
        m = jnp.max(s, axis=0, keepdims=True)
        p = jnp.exp2(s - m)
        l = jnp.sum(p, axis=0, keepdims=True)
        ot = _dot(vt_ref[...], p.astype(BF16)) / l
        o_ref[t * tq:(t + 1) * tq, :] = ot.T.astype(BF16)


def _mla_attn(q, k, v, kx, vx, *, nc, dseq, dbatch, past, tq):
    off = nc // dseq
    kern = functools.partial(_mla_attn_kernel, tq=tq)
    return pl.pallas_call(
        kern,
        out_shape=jax.ShapeDtypeStruct((dbatch * dseq, H_C * V_DIM), BF16),
        grid=(dbatch, H_C),
        in_specs=[
            pl.BlockSpec((dseq, QK_PAD), lambda b, h: (off + b, h)),
            pl.BlockSpec((dseq, QK_PAD), lambda b, h: (off + b, h)),
            pl.BlockSpec((dseq, V_DIM), lambda b, h: (off + b, h)),
            pl.BlockSpec((past, QK_PAD), lambda b, h: (b, h)),
            pl.BlockSpec((past, V_DIM), lambda b, h: (b, h)),
        ],
        out_specs=pl.BlockSpec((dseq, V_DIM), lambda b, h: (b, h)),
        scratch_shapes=[pltpu.VMEM((past + dseq, QK_PAD), BF16),
                        pltpu.VMEM((V_DIM, past + dseq), BF16)],
        compiler_params=_params("arbitrary", "arbitrary"),
        name="mla_attn",
    )(q, k, v, kx, vx)


def _rope_tables(dseq, tm):
    t = np.arange(dseq)
    quarter = ROPE // 4
    freqs = jnp.asarray(ROPE_BASE, F32) ** (-jnp.arange(quarter, dtype=F32) / quarter)
    ang_r = jnp.asarray(t // GRID_W, F32)[:, None] * freqs
    ang_c = jnp.asarray(t % GRID_W, F32)[:, None] * freqs
    zeros = jnp.zeros((dseq, LANES - ROPE), F32)
    cos = jnp.concatenate([jnp.cos(ang_r), jnp.cos(ang_r), jnp.cos(ang_c), jnp.cos(ang_c), zeros], axis=1)
    sin = jnp.concatenate([-jnp.sin(ang_r), jnp.sin(ang_r), -jnp.sin(ang_c), jnp.sin(ang_c), zeros], axis=1)
    ident = jnp.concatenate([jnp.ones((tm, ROPE), F32), jnp.zeros((tm, LANES - ROPE), F32)], axis=1)
    return (jnp.concatenate([ident, cos], axis=0),
            jnp.concatenate([jnp.zeros((tm, LANES), F32), sin], axis=0))


def _pick_tile(pref, *lengths):
    t = pref
    while any(l % t for l in lengths):
        t //= 2
    return t


def kernel(x_prompt, x_sample, cache_nat_k, cache_nat_v, cache_mla_ckv, cache_mla_krope, c, c_ctx,
           norm1_g, norm2_g, w_ada, b_ada, w_in_ab, g_qn_a, g_kn_a, rpb_a, conv_b_w, w_out_ab,
           w_down_c, g_cq, g_ckv, w_uq_c, w_ukv_c, g_qn_c, g_kn_c, w_o_c, w_ffn_in, w_ffn_out):
    batch, seq, d = x_prompt.shape
    dbatch, dseq, _ = x_sample.shape
    depth = w_ada.shape[0]
    n_even = w_in_ab.shape[0]
    past = cache_nat_k.shape[3]
    nc, ns = batch * seq, dbatch * dseq
    n = nc + ns
    dff = w_ffn_out.shape[1]
    assert nc % dseq == 0 and dseq % seq == 0 and dseq % GRID_W == 0

    tm = _pick_tile(512, nc, dseq)
    tm_ab = _pick_tile(256, nc, dseq)
    tm_c = _pick_tile(512, nc, dseq)
    tm_f = _pick_tile(1024, nc, dseq)
    tf = _pick_tile(512, dff)
    assert tm % seq == 0

    x, x2, second = x_prompt.reshape(nc, d), x_sample.reshape(ns, d), "select"

    groups = 1 + dbatch
    gp = -(-groups // SUBLANES) * SUBLANES
    cond = jnp.concatenate([c_ctx[None, :], c, jnp.zeros((gp - groups, d), F32)], axis=0)
    mods_all = _adaln(cond, w_ada, b_ada).reshape(depth, gp, 6, d)

    rope_c, rope_s = _rope_tables(dseq, tm_c)
    ident_c = jnp.concatenate([jnp.ones((past, ROPE), F32), jnp.zeros((past, LANES - ROPE), F32)], axis=1)
    shape_kw = dict(nc=nc, dseq=dseq)

    w_in_ab_b = w_in_ab.astype(BF16)
    w_out_ab_b = w_out_ab.astype(BF16)
    w_o_c_b = w_o_c.astype(BF16)
    w_ffn_b = (w_ffn_in[0].astype(BF16), w_ffn_out[0].astype(BF16))

    nat = None
    mla_ckv, mla_kr = [], []
    for l in range(depth):
        i = l // 2
        mods = mods_all[l]
        if l % 2 == 0:
            gq = (g_qn_a[i] * (DH_A ** -0.5 * LOG2E))[None, :]
            qkv, gb, z, nk, nv = _inproj_ab(
                x, x2, second, mods, norm1_g[l][None, :], w_in_ab_b, gq, g_kn_a[i][None, :], nat, i,
                seq=seq, batch=batch, n_even=n_even, tm=tm_ab, **shape_kw)
            nat = (nk, nv)
            ya_c = _ctx_attn(qkv, qkv, qkv, (0, 1, 2), nc=nc, batch=batch, seq=seq,
                             heads=H_A, dq=DH_A, dv=DH_A)
            ya_s = _nat_attn(rpb_a[i], qkv, cache_nat_k, cache_nat_v, i, dbatch=dbatch, **shape_kw)
            x, h2 = _outproj(x, x2, second, mods, norm2_g[l][None, :], ya_c, ya_s, w_out_ab_b, i,
                             (gb, z, conv_b_w[i]), seq=seq, tm=tm, **shape_kw)
        else:
            wd = jnp.pad(w_down_c[i], ((0, 0), (0, LANES - ROPE))).astype(BF16)
            wq = jnp.pad(w_uq_c[i].reshape(Q_LORA, H_C, QK_DIM),
                         ((0, 0), (0, 0), (0, QK_PAD - QK_DIM))).reshape(Q_LORA, H_C * QK_PAD).astype(BF16)
            wkv = w_ukv_c[i].astype(BF16)
            gq = jnp.pad(g_qn_c[i] * (QK_DIM ** -0.5 * LOG2E), (0, QK_PAD - QK_DIM))[None, :]
            gkn = g_kn_c[i][None, :NOPE]
            gkr = jnp.pad(g_kn_c[i][NOPE:], (0, LANES - ROPE))[None, :]
            q, k, v, ckv_n, kr = _inproj_c(
                x, x2, second, mods, norm1_g[l][None, :], wd, g_cq[i][None, :], g_ckv[i][None, :],
                wq, gq, wkv, gkn, gkr, rope_c, rope_s, tm=tm_c, **shape_kw)
            mla_ckv.append(ckv_n.reshape(batch, seq, KV_LORA))
            mla_kr.append(kr.reshape(batch, seq, ROPE))
            kx, vx = _cache_kv(
                cache_mla_ckv[:, i].reshape(dbatch * past, KV_LORA),
                jnp.pad(cache_mla_krope[:, i].reshape(dbatch * past, ROPE), ((0, 0), (0, LANES - ROPE))),
                wkv, gkn, gkr, ident_c, jnp.zeros((past, LANES), F32), tm=past)
            ya_c = _ctx_attn(q, k, v, (0, 0, 0), nc=nc, batch=batch, seq=seq,
                             heads=H_C, dq=QK_PAD, dv=V_DIM)
            ya_s = _mla_attn(q, k, v, kx, vx, dbatch=dbatch, past=past,
                             tq=_pick_tile(512, dseq), **shape_kw)
            x, h2 = _outproj(x, x2, second, mods, norm2_g[l][None, :], ya_c, ya_s, w_o_c_b, i,
                             seq=seq, tm=tm, **shape_kw)
        last = l == depth - 1
        res = _ffn(x, h2, mods, w_ffn_b[0], w_ffn_b[1], None if last else (w_ffn_in, w_ffn_out), l,
                   last=last, tm=tm if last else tm_f, tf=tf, **shape_kw)
        if last:
            x2 = res
        else:
            x2, second, w_ffn_b = res[0], "add", res[1:]

    return (x2[0].reshape(batch, seq, d), x2[1].reshape(dbatch, dseq, d), nat[0], nat[1],
            jnp.stack(mla_ckv, axis=1), jnp.stack(mla_kr, axis=1))
```

```python
import functools

import numpy as np
import jax
import jax.numpy as jnp
from jax import lax
from jax.experimental import pallas as pl
from jax.experimental.pallas import tpu as pltpu

F32 = jnp.float32
BF16 = jnp.bfloat16

EPS = 1e-6
GRID_W = 64
WIN_R = 8
WIN_C = 16
ROPE_BASE = 10000.0
H_A = 8
DH_A = 128
H_C = 16
Q_LORA = 512
KV_LORA = 256
NOPE = 128
ROPE = 64
V_DIM = 128
QK_DIM = NOPE + ROPE
QK_PAD = 256
LANES = 128
SUBLANES = 8
HALO = 16
NEG = -1e30
LOG2E = 1.4426950408889634

VMEM_LIMIT = 56 * 1024 * 1024
ROW_BLOCK = 128
ROW_BLOCK_AB = 256
ROW_BLOCK_C = 256
VMEM_LIMIT_C = 61 * 1024 * 1024


def _params(*sem, vmem=VMEM_LIMIT):
    return pltpu.CompilerParams(dimension_semantics=sem, vmem_limit_bytes=vmem)


def _group(i, nct, per):
    return jnp.where(i < nct, 0, 1 + (i - nct) // per)


def _dot(a, b):
    return jnp.dot(a, b, preferred_element_type=F32)


def _dot_nt(a, b):
    return lax.dot_general(a, b, (((1,), (1,)), ((), ())), preferred_element_type=F32)


def _ada_kernel(c_ref, w_ref, b_ref, o_ref):
    c = c_ref[...]
    s = (c / (1.0 + jnp.exp(-c))).astype(BF16)
    o_ref[...] = _dot(s, w_ref[...].astype(BF16)) + b_ref[...]


def _adaln(cond, w_ada, b_ada):
    depth, d, n6 = w_ada.shape
    gp = cond.shape[0]
    tn = 1024 if n6 % 1024 == 0 else n6
    return pl.pallas_call(
        _ada_kernel,
        out_shape=jax.ShapeDtypeStruct((depth, gp, n6), F32),
        grid=(depth, n6 // tn),
        in_specs=[
            pl.BlockSpec((gp, d), lambda l, n: (0, 0)),
            pl.BlockSpec((None, d, tn), lambda l, n: (l, 0, n)),
            pl.BlockSpec((None, 1, tn), lambda l, n: (l, 0, n)),
        ],
        out_specs=pl.BlockSpec((None, gp, tn), lambda l, n: (l, 0, n)),
        compiler_params=_params("arbitrary", "arbitrary"),
        name="adaln",
    )(cond, w_ada, b_ada.reshape(depth, 1, n6))


def _row_spec(tm, d, nct, second, which):
    if second != "select":
        return pl.BlockSpec((tm, d), lambda i, *_: (i, 0))
    if which == 0:
        return pl.BlockSpec((tm, d), lambda i, *_: (jnp.minimum(i, nct - 1), 0))
    return pl.BlockSpec((tm, d), lambda i, *_: (jnp.maximum(i - nct, 0), 0))


def _residual(x_ref, x2_ref, rows, second, is_ctx):
    x = x_ref[rows, :]
    if second == "add":
        x = x + x2_ref[rows, :]
    elif second == "select":
        x = jnp.where(is_ctx, x, x2_ref[rows, :])
    return x


def _inproj_ab_kernel(x_ref, *rest, nct, seq, first, rb, second):
    x2_ref = None
    if second:
        x2_ref, *rest = rest
    mod_ref, g1_ref, w_ref, gq_ref, gk_ref, *rest = rest
    if first:
        qkv_ref, gb_ref, z_ref, natk_ref, natv_ref = rest
    else:
        _, _, qkv_ref, gb_ref, z_ref, natk_ref, natv_ref = rest
    i = pl.program_id(0)
    tm = x_ref.shape[0]
    na = H_A * DH_A
    is_ctx = i < nct
    gs = g1_ref[...] * (1.0 + mod_ref[1:2, :])

    def nat_store(nat_ref, rows, hh, val):
        bb, off = rows.start // seq, rows.start % seq
        if first:
            nat_ref[bb, 0, hh, off:off + rb, :] = val
        else:
            nat_ref[bb, hh, off:off + rb, :] = val

    def head_norm(rows, r, col, gain_ref, nat_ref):
        for hh in range(H_A):
            t = r[:, hh * DH_A:(hh + 1) * DH_A]
            ms = jnp.mean(t * t, axis=-1, keepdims=True)
            val = t * lax.rsqrt(ms + EPS) * gain_ref[...]
            qkv_ref[rows, col * na + hh * DH_A:col * na + (hh + 1) * DH_A] = val.astype(BF16)
            if nat_ref is not None:
                nat_store(nat_ref, rows, hh, val)

    def tile(ctx):
        if ctx and first and natk_ref.shape[1] > 1:
            natk_ref[:, 1:] = jnp.zeros_like(natk_ref[:, 1:])
            natv_ref[:, 1:] = jnp.zeros_like(natv_ref[:, 1:])
        for r0 in range(0, tm, rb):
            rows = slice(r0, r0 + rb)
            x = _residual(x_ref, x2_ref, rows, second, is_ctx)
            ms = jnp.mean(x * x, axis=-1, keepdims=True)
            h = (x * lax.rsqrt(ms + EPS) * gs + mod_ref[0:1, :]).astype(BF16)

            def proj(col):
                return _dot(h, w_ref[:, col * na:(col + 1) * na])

            head_norm(rows, proj(0), 0, gq_ref, None)
            head_norm(rows, proj(1), 1, gk_ref, natk_ref if ctx else None)
            v = proj(2)
            qkv_ref[rows, 2 * na:3 * na] = v.astype(BF16)
            if ctx:
                for hh in range(H_A):
                    nat_store(natv_ref, rows, hh, v[:, hh * DH_A:(hh + 1) * DH_A])
            gb_ref[rows, :] = proj(3).astype(BF16)
            gc = proj(4)
            z_ref[rows, :] = (gc * proj(5)).astype(BF16)

    pl.when(is_ctx)(lambda: tile(True))
    pl.when(jnp.logical_not(is_ctx))(lambda: tile(False))


def _inproj_ab(x, x2, second, mods, g1, w_in, gq, gk, nat_prev, slot,
               *, nc, seq, dseq, batch, n_even, tm):
    d = x.shape[1]
    n = nc + (x2.shape[0] if second == "select" else x.shape[0] - nc)
    na = H_A * DH_A
    nct = nc // tm
    per = dseq // tm
    first = nat_prev is None
    kern = functools.partial(_inproj_ab_kernel, nct=nct, seq=seq, first=first,
                             rb=_pick_tile(ROW_BLOCK_AB, tm, seq), second=second)
    nat_shape = jax.ShapeDtypeStruct((batch, n_even, H_A, seq, DH_A), F32)
    if first:
        nat_spec = pl.BlockSpec((tm // seq, n_even, H_A, seq, DH_A),
                                lambda i: (jnp.minimum(i, nct - 1), 0, 0, 0, 0))
    else:
        nat_spec = pl.BlockSpec((tm // seq, None, H_A, seq, DH_A),
                                lambda i: (jnp.minimum(i, nct - 1), slot, 0, 0, 0))
    in_specs = [_row_spec(tm, d, nct, second, 0)] + ([_row_spec(tm, d, nct, second, 1)] if second else []) + [
        pl.BlockSpec((None, 6, d), lambda i: (_group(i, nct, per), 0, 0)),
        pl.BlockSpec((1, d), lambda i: (0, 0)),
        pl.BlockSpec((None,) + w_in.shape[1:], lambda i: (slot, 0, 0), pipeline_mode=pl.Buffered(1)),
        pl.BlockSpec((1, DH_A), lambda i: (0, 0)),
        pl.BlockSpec((1, DH_A), lambda i: (0, 0)),
    ]
    args = [x] + ([x2] if second else []) + [mods, g1, w_in, gq, gk]
    aliases = {}
    if not first:
        in_specs += [pl.BlockSpec(memory_space=pl.ANY)] * 2
        aliases = {len(args): 3, len(args) + 1: 4}
        args += list(nat_prev)
    return pl.pallas_call(
        kern,
        out_shape=(jax.ShapeDtypeStruct((n, 3 * na), BF16),
                   jax.ShapeDtypeStruct((n, na), BF16),
                   jax.ShapeDtypeStruct((n, na), BF16),
                   nat_shape, nat_shape),
        grid=(n // tm,),
        in_specs=in_specs,
        out_specs=(
            pl.BlockSpec((tm, 3 * na), lambda i: (i, 0)),
            pl.BlockSpec((tm, na), lambda i: (i, 0)),
            pl.BlockSpec((tm, na), lambda i: (i, 0)),
            nat_spec, nat_spec,
        ),
        input_output_aliases=aliases,
        compiler_params=_params("arbitrary"),
        name="inproj_ab",
    )(*args)


def _ctx_attn_kernel(q_ref, k_ref, v_ref, o_ref, *, heads, dq, dv, seq):
    pairs = [(slice(r0, r0 + seq), hh) for r0 in range(0, q_ref.shape[0], seq) for hh in range(heads)]

    def scores(rows, hh):
        return _dot_nt(q_ref[rows, hh * dq:(hh + 1) * dq], k_ref[rows, hh * dq:(hh + 1) * dq])

    s_next = scores(*pairs[0])
    for idx, (rows, hh) in enumerate(pairs):
        s = s_next
        if idx + 1 < len(pairs):
            s_next = scores(*pairs[idx + 1])
        m = jnp.max(s, axis=-1, keepdims=True)
        p = jnp.exp2(s - m)
        l = jnp.sum(p, axis=-1, keepdims=True)
        o = _dot(p.astype(BF16), v_ref[rows, hh * dv:(hh + 1) * dv]) / l
        o_ref[rows, hh * dv:(hh + 1) * dv] = o.astype(BF16)


def _ctx_attn(q_arr, k_arr, v_arr, cols, *, nc, batch, seq, heads, dq, dv):
    cq, ck, cv = cols
    reqs = 2 if batch % 2 == 0 else 1
    rows = reqs * seq
    kern = functools.partial(_ctx_attn_kernel, heads=heads, dq=dq, dv=dv, seq=seq)
    return pl.pallas_call(
        kern,
        out_shape=jax.ShapeDtypeStruct((nc, heads * dv), BF16),
        grid=(batch // reqs,),
        in_specs=[
            pl.BlockSpec((rows, heads * dq), lambda b: (b, cq)),
            pl.BlockSpec((rows, heads * dq), lambda b: (b, ck)),
            pl.BlockSpec((rows, heads * dv), lambda b: (b, cv)),
        ],
        out_specs=pl.BlockSpec((rows, heads * dv), lambda b: (b, 0)),
        compiler_params=_params("arbitrary"),
        name="ctx_attn",
    )(q_arr, k_arr, v_arr)


NAT_QR = 8
NAT_KR = 16


def _nat_window(a, rows):
    return int(np.clip(a * NAT_QR - WIN_R // 2, 0, rows - NAT_KR))


def _nat_attn_kernel(rpb_ref, q_ref, k_ref, v_ref, ck_ref, cv_ref, o_ref,
                     t2_ref, bias_ref, *, rows):
    hh = pl.program_id(0)
    b = pl.program_id(1)
    nblk = rows // NAT_QR
    n_dr = 2 * WIN_R - 1
    n_dc = 2 * WIN_C - 1

    @pl.when(b == 0)
    def _():
        j = lax.broadcasted_iota(jnp.int32, (GRID_W, LANES), 0)
        kc = lax.broadcasted_iota(jnp.int32, (GRID_W, LANES), 1) % GRID_W
        cs = jnp.clip(j - WIN_C // 2, 0, GRID_W - WIN_C)
        col_in = (kc >= cs) & (kc < cs + WIN_C)
        dc = kc - j + (WIN_C - 1)
        for d in range(n_dr):
            t = jnp.full((GRID_W, LANES), NEG, F32)
            for c in range(n_dc):
                t = jnp.where(col_in & (dc == c), rpb_ref[hh, d * n_dc + c] * LOG2E, t)
            t2_ref[d] = t
        lane = lax.broadcasted_iota(jnp.int32, (GRID_W, LANES), 1)
        negt = jnp.full((GRID_W, LANES), NEG, F32)
        for a in range(nblk):
            ws = _nat_window(a, rows)
            for qr in range(NAT_QR):
                r = a * NAT_QR + qr
                rs = int(np.clip(r - WIN_R // 2, 0, rows - WIN_R))
                for pp in range(NAT_KR // 2):
                    halves = []
                    for kk in (2 * pp, 2 * pp + 1):
                        kr = ws + kk
                        halves.append(t2_ref[kr - r + WIN_R - 1] if rs <= kr < rs + WIN_R else None)
                    if halves[0] is None and halves[1] is None:
                        blk = negt
                    else:
                        left = negt if halves[0] is None else halves[0]
                        right = negt if halves[1] is None else halves[1]
                        blk = jnp.where(lane < GRID_W, left, right)
                    bias_ref[a, qr * GRID_W:(qr + 1) * GRID_W, pp * LANES:(pp + 1) * LANES] = blk

    ck = ck_ref[...].astype(BF16)
    cv = cv_ref[...].astype(BF16)
    tq = NAT_QR * GRID_W
    tk = NAT_KR * GRID_W

    def scores(a):
        ws = _nat_window(a, rows)
        q = q_ref[a * tq:(a + 1) * tq, :]
        kw = k_ref[ws * GRID_W:ws * GRID_W + tk, :]
        return _dot_nt(q, kw) + bias_ref[a], _dot_nt(q, ck)

    s_next = scores(0)
    for a in range(nblk):
        s_loc, s_ctx = s_next
        if a + 1 < nblk:
            s_next = scores(a + 1)
        ws = _nat_window(a, rows)
        vw = v_ref[ws * GRID_W:ws * GRID_W + tk, :]
        m = jnp.maximum(jnp.max(s_loc, axis=-1, keepdims=True),
                        jnp.max(s_ctx, axis=-1, keepdims=True))
        p_loc = jnp.exp2(s_loc - m)
        p_ctx = jnp.exp2(s_ctx - m)
        l = jnp.sum(p_loc, axis=-1, keepdims=True) + jnp.sum(p_ctx, axis=-1, keepdims=True)
        o = (_dot(p_loc.astype(BF16), vw) + _dot(p_ctx.astype(BF16), cv)) / l
        o_ref[a * tq:(a + 1) * tq, :] = o.astype(BF16)


def _nat_attn(rpb, qkv, cache_k, cache_v, slot, *, nc, dseq, dbatch):
    rows = dseq // GRID_W
    assert rows % NAT_QR == 0 and rows >= NAT_KR
    nblk = rows // NAT_QR
    past = cache_k.shape[3]
    off = nc // dseq
    n_dr, n_dc = 2 * WIN_R - 1, 2 * WIN_C - 1
    kern = functools.partial(_nat_attn_kernel, rows=rows)
    cache_spec = pl.BlockSpec((None, None, None, past, DH_A), lambda h, b: (b, slot, h, 0, 0))
    return pl.pallas_call(
        kern,
        out_shape=jax.ShapeDtypeStruct((dbatch * dseq, H_A * DH_A), BF16),
        grid=(H_A, dbatch),
        in_specs=[
            pl.BlockSpec(memory_space=pltpu.SMEM),
            pl.BlockSpec((dseq, DH_A), lambda h, b: (off + b, h)),
            pl.BlockSpec((dseq, DH_A), lambda h, b: (off + b, H_A + h)),
            pl.BlockSpec((dseq, DH_A), lambda h, b: (off + b, 2 * H_A + h)),
            cache_spec, cache_spec,
        ],
        out_specs=pl.BlockSpec((dseq, DH_A), lambda h, b: (b, h)),
        scratch_shapes=[pltpu.VMEM((n_dr, GRID_W, LANES), F32),
                        pltpu.VMEM((nblk, NAT_QR * GRID_W, NAT_KR * GRID_W), F32)],
        compiler_params=_params("arbitrary", "arbitrary"),
        name="nat_attn",
    )(rpb.reshape(H_A, n_dr * n_dc), qkv, qkv, qkv, cache_k, cache_v)


def _outproj_kernel(x_ref, *refs, conv, nct, seq, dseq, rb, second):
    x2_ref = None
    if second:
        x2_ref, *refs = refs
    if conv:
        (mod_ref, g2_ref, yac_ref, yas_ref, gb_ref, z_ref, zp_ref, zn_ref, cw_ref, w_ref,
         o_ref, h2_ref, zs_ref) = refs
    else:
        mod_ref, g2_ref, yac_ref, yas_ref, w_ref, o_ref, h2_ref = refs
    i = pl.program_id(0)
    tm = x_ref.shape[0]
    ka = yac_ref.shape[1]
    if conv:
        zs_ref[0:HALO, :] = zp_ref[...].astype(F32)
        zs_ref[HALO:HALO + tm, :] = z_ref[...].astype(F32)
        zs_ref[HALO + tm:2 * HALO + tm, :] = zn_ref[...].astype(F32)
        last = jnp.where(i < nct, seq - 1, dseq - 1)
    gate = mod_ref[2:3, :]
    gs = g2_ref[...] * (1.0 + mod_ref[4:5, :])
    shift = mod_ref[3:4, :]
    for r0 in range(0, tm, rb):
        rows = slice(r0, r0 + rb)
        ya = jnp.where(i < nct, yac_ref[rows, :], yas_ref[rows, :])
        y = _dot(ya, w_ref[0:ka, :])
        if conv:
            row = r0 + lax.broadcasted_iota(jnp.int32, (rb, 1), 0)
            pos = jnp.where(i < nct, row % seq, ((i - nct) * tm + row) % dseq)
            z_prev = jnp.where(pos == 0, 0.0, zs_ref[HALO - 1 + r0:HALO - 1 + r0 + rb, :])
            z_next = jnp.where(pos == last, 0.0, zs_ref[HALO + 1 + r0:HALO + 1 + r0 + rb, :])
            z_mid = zs_ref[HALO + r0:HALO + r0 + rb, :]
            cv = cw_ref[0:1, :] * z_prev + cw_ref[1:2, :] * z_mid + cw_ref[2:3, :] * z_next
            yb = gb_ref[rows, :].astype(F32) * cv
            y = y + _dot(yb.astype(BF16), w_ref[ka:, :])
        xn = _residual(x_ref, x2_ref, rows, second, i < nct) + gate * y
        o_ref[rows, :] = xn
        ms = jnp.mean(xn * xn, axis=-1, keepdims=True)
        h2_ref[rows, :] = (xn * lax.rsqrt(ms + EPS) * gs + shift).astype(BF16)


def _outproj(x, x2, second, mods, g2, ya_c, ya_s, w_out, slot, conv_in=None, *, nc, seq, dseq, tm):
    d = x.shape[1]
    n = nc + ya_s.shape[0]
    ka = ya_c.shape[1]
    nct = nc // tm
    per = dseq // tm
    conv = conv_in is not None
    kern = functools.partial(_outproj_kernel, conv=conv, nct=nct, seq=seq, dseq=dseq,
                             rb=_pick_tile(ROW_BLOCK, tm), second=second)
    in_specs = [_row_spec(tm, d, nct, second, 0)] + ([_row_spec(tm, d, nct, second, 1)] if second else []) + [
        pl.BlockSpec((None, 6, d), lambda i: (_group(i, nct, per), 0, 0)),
        pl.BlockSpec((1, d), lambda i: (0, 0)),
        pl.BlockSpec((tm, ka), lambda i: (jnp.minimum(i, nct - 1), 0)),
        pl.BlockSpec((tm, ka), lambda i: (jnp.maximum(i - nct, 0), 0)),
    ]
    args = [x] + ([x2] if second else []) + [mods, g2, ya_c, ya_s]
    scratch = []
    if conv:
        gb, z, cw = conv_in
        nb = z.shape[1]
        th = tm // HALO
        nh = n // HALO
        in_specs += [
            pl.BlockSpec((tm, nb), lambda i: (i, 0)),
            pl.BlockSpec((tm, nb), lambda i: (i, 0)),
            pl.BlockSpec((HALO, nb), lambda i: (jnp.maximum(i * th - 1, 0), 0)),
            pl.BlockSpec((HALO, nb), lambda i: (jnp.minimum((i + 1) * th, nh - 1), 0)),
            pl.BlockSpec((3, nb), lambda i: (0, 0)),
        ]
        args += [gb, z, z, z, cw]
        scratch = [pltpu.VMEM((tm + 2 * HALO, nb), F32)]
    in_specs.append(pl.BlockSpec((None,) + w_out.shape[1:], lambda i: (slot, 0, 0),
                                 pipeline_mode=pl.Buffered(1)))
    args.append(w_out)
    return pl.pallas_call(
        kern,
        out_shape=(jax.ShapeDtypeStruct((n, d), F32), jax.ShapeDtypeStruct((n, d), BF16)),
        grid=(n // tm,),
        in_specs=in_specs,
        out_specs=(pl.BlockSpec((tm, d), lambda i: (i, 0)), pl.BlockSpec((tm, d), lambda i: (i, 0))),
        scratch_shapes=scratch,
        compiler_params=_params("arbitrary"),
        name="outproj_conv" if conv else "outproj",
    )(*args)


def _ffn_kernel(*refs, last, nct, cast_next):
    if last:
        x_ref, h_ref, mod_ref, wg_ref, wu_ref, wo_ref, oc_ref, os_ref, acc_ref = refs
    elif cast_next:
        (h_ref, mod_ref, wg_ref, wu_ref, wo_ref, nin_ref, nout_ref,
         acc_ref, nin_b_ref, nout_b_ref) = refs
    else:
        h_ref, mod_ref, wg_ref, wu_ref, wo_ref, acc_ref = refs
    i = pl.program_id(0)
    f = pl.program_id(1)

    def step(start):
        h = h_ref[...]
        g = _dot(h, wg_ref[...])
        u = _dot(h, wu_ref[...])
        a = (g / (1.0 + jnp.exp(-g))) * u
        part = _dot(a.astype(BF16), wo_ref[...])
        acc_ref[...] = part if start else acc_ref[...] + part
        if cast_next:
            nin_b_ref[...] = nin_ref[...].astype(BF16)
            nout_b_ref[...] = nout_ref[...].astype(BF16)

    pl.when(f == 0)(lambda: step(True))
    pl.when(f != 0)(lambda: step(False))

    @pl.when(f == pl.num_programs(1) - 1)
    def _():
        delta = mod_ref[5:6, :] * acc_ref[...]
        if last:
            @pl.when(i < nct)
            def _():
                oc_ref[...] = x_ref[...] + delta

            @pl.when(i >= nct)
            def _():
                os_ref[...] = x_ref[...] + delta
        else:
            acc_ref[...] = delta


def _slab_rows(total, steps):
    r = HALO
    while total % r or total // r > steps:
        r += HALO
    return r


def _ffn(x, h2, mods, w_in, w_out, next_w, layer, *, last, nc, dseq, tm, tf):
    n, d = h2.shape
    dff = w_out.shape[0]
    nf = dff // tf
    nct = nc // tm
    per = dseq // tm
    cast_next = next_w is not None
    kern = functools.partial(_ffn_kernel, last=last, nct=nct, cast_next=cast_next)
    row_spec = pl.BlockSpec((tm, d), lambda i, f: (i, 0))
    in_specs = [
        row_spec,
        pl.BlockSpec((None, 6, d), lambda i, f: (_group(i, nct, per), 0, 0)),
        pl.BlockSpec((d, tf), lambda i, f: (0, f)),
        pl.BlockSpec((d, tf), lambda i, f: (0, nf + f)),
        pl.BlockSpec((tf, d), lambda i, f: (f, 0)),
    ]
    args = [h2, mods, w_in, w_in, w_out]
    scratch = []
    if last:
        in_specs.insert(0, row_spec)
        args.insert(0, x)
        out_shape = (jax.ShapeDtypeStruct((nc, d), F32), jax.ShapeDtypeStruct((n - nc, d), F32))
        out_specs = (pl.BlockSpec((tm, d), lambda i, f: (jnp.minimum(i, nct - 1), 0)),
                     pl.BlockSpec((tm, d), lambda i, f: (jnp.maximum(i - nct, 0), 0)))
        scratch = [pltpu.VMEM((tm, d), F32)]
    else:
        out_shape = jax.ShapeDtypeStruct((n, d), F32)
        out_specs = row_spec
    if cast_next:
        steps = (n // tm) * nf
        r_in, r_out = _slab_rows(d, steps), _slab_rows(dff, steps)
        slab = lambda nslab: (lambda i, f: (jnp.minimum(i * nf + f, nslab - 1), 0))
        slab3 = lambda nslab: (lambda i, f: (layer + 1, jnp.minimum(i * nf + f, nslab - 1), 0))
        in_specs += [pl.BlockSpec((None, r_in, 2 * dff), slab3(d // r_in)),
                     pl.BlockSpec((None, r_out, d), slab3(dff // r_out))]
        args += list(next_w)
        out_shape = (out_shape, jax.ShapeDtypeStruct((d, 2 * dff), BF16),
                     jax.ShapeDtypeStruct((dff, d), BF16))
        out_specs = (out_specs, pl.BlockSpec((r_in, 2 * dff), slab(d // r_in)),
                     pl.BlockSpec((r_out, d), slab(dff // r_out)))
    return pl.pallas_call(
        kern,
        out_shape=out_shape,
        grid=(n // tm, nf),
        in_specs=in_specs,
        out_specs=out_specs,
        scratch_shapes=scratch,
        compiler_params=_params("arbitrary", "arbitrary"),
        name="ffn",
    )(*args)


def _rope_low_half(rows):
    lane = lax.broadcasted_iota(jnp.int32, (rows, LANES), 1)
    return lane % (ROPE // 2) < ROPE // 4


def _rope(x, c, s, low):
    quarter = ROPE // 4
    swapped = jnp.where(low, pltpu.roll(x, LANES - quarter, 1), pltpu.roll(x, quarter, 1))
    return x * c + swapped * s


def _mla_kv_expand(ckv_bf16, krp, wkv_ref, gkn_ref, gkr_ref, c, s, low, k_ref, v_ref, rows=slice(None)):
    kv = _dot(ckv_bf16, wkv_ref[...])
    krr = _rope(krp * gkr_ref[...], c, s, low)
    ss_kr = jnp.sum(krp * krp, axis=-1, keepdims=True)
    for hh in range(H_C):
        base = hh * (NOPE + V_DIM)
        kn = kv[:, base:base + NOPE]
        rstd = lax.rsqrt((jnp.sum(kn * kn, axis=-1, keepdims=True) + ss_kr) / QK_DIM + EPS)
        k_ref[rows, hh * QK_PAD:hh * QK_PAD + NOPE] = (kn * rstd * gkn_ref[...]).astype(BF16)
        k_ref[rows, hh * QK_PAD + NOPE:(hh + 1) * QK_PAD] = (krr * rstd).astype(BF16)
        v_ref[rows, hh * V_DIM:(hh + 1) * V_DIM] = kv[:, base + NOPE:base + NOPE + V_DIM].astype(BF16)


def _inproj_c_kernel(x_ref, *rest, nct, rb, second):
    x2_ref = None
    if second:
        x2_ref, *rest = rest
    (mod_ref, g1_ref, wd_ref, gcq_ref, gckv_ref, wq_ref, gq_ref, wkv_ref, gkn_ref, gkr_ref,
     c_ref, s_ref, q_ref, k_ref, v_ref, ckv_ref, kr_ref) = rest
    i = pl.program_id(0)
    tm = x_ref.shape[0]
    blocks = [slice(r * rb, (r + 1) * rb) for r in range(tm // rb)]
    gs = g1_ref[...] * (1.0 + mod_ref[1:2, :])
    low = _rope_low_half(rb)

    def latents(rows):
        x = _residual(x_ref, x2_ref, rows, second, i < nct)
        ms = jnp.mean(x * x, axis=-1, keepdims=True)
        h = (x * lax.rsqrt(ms + EPS) * gs + mod_ref[0:1, :]).astype(BF16)
        dn = _dot(h, wd_ref[...])
        cq = dn[:, :Q_LORA]
        ckv = dn[:, Q_LORA:Q_LORA + KV_LORA]
        krp = dn[:, Q_LORA + KV_LORA:]
        cqn = cq * lax.rsqrt(jnp.mean(cq * cq, axis=-1, keepdims=True) + EPS) * gcq_ref[...]
        ckvn = ckv * lax.rsqrt(jnp.mean(ckv * ckv, axis=-1, keepdims=True) + EPS) * gckv_ref[...]
        return cqn.astype(BF16), ckvn, krp

    def expand(rows, cqn, ckvn, krp, ctx):
        if ctx:
            ckv_ref[rows, :] = ckvn
            kr_ref[rows, :] = krp[:, :ROPE]
        c = c_ref[rows, :]
        s = s_ref[rows, :]
        q = _dot(cqn, wq_ref[...])
        for hh in range(H_C):
            qh = q[:, hh * QK_PAD:(hh + 1) * QK_PAD]
            rstd = lax.rsqrt(jnp.sum(qh * qh, axis=-1, keepdims=True) / QK_DIM + EPS)
            qn = qh * rstd * gq_ref[...]
            q_ref[rows, hh * QK_PAD:hh * QK_PAD + NOPE] = qn[:, :NOPE].astype(BF16)
            q_ref[rows, hh * QK_PAD + NOPE:(hh + 1) * QK_PAD] = _rope(qn[:, NOPE:], c, s, low).astype(BF16)
        _mla_kv_expand(ckvn.astype(BF16), krp, wkv_ref, gkn_ref, gkr_ref, c, s, low, k_ref, v_ref, rows)

    def tile(ctx):
        nxt = latents(blocks[0])
        for r, rows in enumerate(blocks):
            cur = nxt
            if r + 1 < len(blocks):
                nxt = latents(blocks[r + 1])
            expand(rows, *cur, ctx)

    pl.when(i < nct)(lambda: tile(True))
    pl.when(i >= nct)(lambda: tile(False))


def _inproj_c(x, x2, second, mods, g1, wd, gcq, gckv, wq, gq, wkv, gkn, gkr, rope_c, rope_s,
              *, nc, dseq, tm):
    d = x.shape[1]
    n = nc + (x2.shape[0] if second == "select" else x.shape[0] - nc)
    nct = nc // tm
    per = dseq // tm
    kern = functools.partial(_inproj_c_kernel, nct=nct, rb=_pick_tile(ROW_BLOCK_C, tm), second=second)
    const = lambda shape: pl.BlockSpec(shape, lambda i: (0,) * len(shape),
                                       pipeline_mode=pl.Buffered(1))
    rope_idx = lambda i: (jnp.where(i < nct, 0, 1 + (i - nct) % per), 0)
    ctx_idx = lambda i: (jnp.minimum(i, nct - 1), 0)
    return pl.pallas_call(
        kern,
        out_shape=(jax.ShapeDtypeStruct((n, H_C * QK_PAD), BF16),
                   jax.ShapeDtypeStruct((n, H_C * QK_PAD), BF16),
                   jax.ShapeDtypeStruct((n, H_C * V_DIM), BF16),
                   jax.ShapeDtypeStruct((nc, KV_LORA), F32),
                   jax.ShapeDtypeStruct((nc, ROPE), F32)),
        grid=(n // tm,),
        in_specs=[_row_spec(tm, d, nct, second, 0)] + ([_row_spec(tm, d, nct, second, 1)] if second else []) + [
            pl.BlockSpec((None, 6, d), lambda i: (_group(i, nct, per), 0, 0)),
            const((1, d)), const(wd.shape), const((1, Q_LORA)), const((1, KV_LORA)),
            const(wq.shape), const((1, QK_PAD)), const(wkv.shape),
            const((1, NOPE)), const((1, LANES)),
            pl.BlockSpec((tm, LANES), rope_idx),
            pl.BlockSpec((tm, LANES), rope_idx),
        ],
        out_specs=(
            pl.BlockSpec((tm, H_C * QK_PAD), lambda i: (i, 0)),
            pl.BlockSpec((tm, H_C * QK_PAD), lambda i: (i, 0)),
            pl.BlockSpec((tm, H_C * V_DIM), lambda i: (i, 0)),
            pl.BlockSpec((tm, KV_LORA), ctx_idx),
            pl.BlockSpec((tm, ROPE), ctx_idx),
        ),
        compiler_params=_params("arbitrary", vmem=VMEM_LIMIT_C),
        name="inproj_c",
    )(*([x] + ([x2] if second else []) + [mods, g1, wd, gcq, gckv, wq, gq, wkv, gkn, gkr, rope_c, rope_s]))


def _cache_kv_kernel(ckv_ref, krp_ref, wkv_ref, gkn_ref, gkr_ref, c_ref, s_ref, k_ref, v_ref):
    _mla_kv_expand(ckv_ref[...].astype(BF16), krp_ref[...], wkv_ref, gkn_ref, gkr_ref,
                   c_ref[...], s_ref[...], _rope_low_half(ckv_ref.shape[0]), k_ref, v_ref)


def _cache_kv(ckv, krp, wkv, gkn, gkr, rope_c, rope_s, *, tm):
    n = ckv.shape[0]
    const = lambda shape: pl.BlockSpec(shape, lambda i: (0,) * len(shape))
    return pl.pallas_call(
        _cache_kv_kernel,
        out_shape=(jax.ShapeDtypeStruct((n, H_C * QK_PAD), BF16),
                   jax.ShapeDtypeStruct((n, H_C * V_DIM), BF16)),
        grid=(n // tm,),
        in_specs=[
            pl.BlockSpec((tm, KV_LORA), lambda i: (i, 0)),
            pl.BlockSpec((tm, LANES), lambda i: (i, 0)),
            const(wkv.shape), const((1, NOPE)), const((1, LANES)),
            const((tm, LANES)), const((tm, LANES)),
        ],
        out_specs=(pl.BlockSpec((tm, H_C * QK_PAD), lambda i: (i, 0)),
                   pl.BlockSpec((tm, H_C * V_DIM), lambda i: (i, 0))),
        compiler_params=_params("arbitrary"),
        name="cache_kv",
    )(ckv, krp, wkv, gkn, gkr, rope_c, rope_s)


def _mla_attn_kernel(q_ref, k_ref, v_ref, kx_ref, vx_ref, o_ref, kall_ref, vt_ref, *, tq):
    past = kx_ref.shape[0]
    kall_ref[0:past, :] = kx_ref[...]
    kall_ref[past:, :] = k_ref[...]
    vt_ref[:, 0:past] = vx_ref[...].astype(F32).T.astype(BF16)
    vt_ref[:, past:] = v_ref[...].astype(F32).T.astype(BF16)

    def scores(t):
        q = q_ref[t * tq:(t + 1) * tq, :]
        return _dot_nt(kall_ref[...], q)

    nblk = q_ref.shape[0] // tq
    s_next = scores(0)
    for t in range(nblk):
        s = s_next
        if t + 1 < nblk:
            s_next = scores(t + 1)
        m = jnp.max(s, axis=0, keepdims=True)
        p = jnp.exp2(s - m)
        l = jnp.sum(p, axis=0, keepdims=True)
        ot = _dot(vt_ref[...], p.astype(BF16)) / l
        o_ref[t * tq:(t + 1) * tq, :] = ot.T.astype(BF16)


def _mla_attn(q, k, v, kx, vx, *, nc, dseq, dbatch, past, tq):
    off = nc // dseq
    kern = functools.partial(_mla_attn_kernel, tq=tq)
    return pl.pallas_call(
        kern,
        out_shape=jax.ShapeDtypeStruct((dbatch * dseq, H_C * V_DIM), BF16),
        grid=(dbatch, H_C),
        in_specs=[
            pl.BlockSpec((dseq, QK_PAD), lambda b, h: (off + b, h)),
            pl.BlockSpec((dseq, QK_PAD), lambda b, h: (off + b, h)),
            pl.BlockSpec((dseq, V_DIM), lambda b, h: (off + b, h)),
            pl.BlockSpec((past, QK_PAD), lambda b, h: (b, h)),
            pl.BlockSpec((past, V_DIM), lambda b, h: (b, h)),
        ],
        out_specs=pl.BlockSpec((dseq, V_DIM), lambda b, h: (b, h)),
        scratch_shapes=[pltpu.VMEM((past + dseq, QK_PAD), BF16),
                        pltpu.VMEM((V_DIM, past + dseq), BF16)],
        compiler_params=_params("arbitrary", "arbitrary"),
        name="mla_attn",
    )(q, k, v, kx, vx)


def _rope_tables(dseq, tm):
    t = np.arange(dseq)
    quarter = ROPE // 4
    freqs = jnp.asarray(ROPE_BASE, F32) ** (-jnp.arange(quarter, dtype=F32) / quarter)
    ang_r = jnp.asarray(t // GRID_W, F32)[:, None] * freqs
    ang_c = jnp.asarray(t % GRID_W, F32)[:, None] * freqs
    zeros = jnp.zeros((dseq, LANES - ROPE), F32)
    cos = jnp.concatenate([jnp.cos(ang_r), jnp.cos(ang_r), jnp.cos(ang_c), jnp.cos(ang_c), zeros], axis=1)
    sin = jnp.concatenate([-jnp.sin(ang_r), jnp.sin(ang_r), -jnp.sin(ang_c), jnp.sin(ang_c), zeros], axis=1)
    ident = jnp.concatenate([jnp.ones((tm, ROPE), F32), jnp.zeros((tm, LANES - ROPE), F32)], axis=1)
    return (jnp.concatenate([ident, cos], axis=0),
            jnp.concatenate([jnp.zeros((tm, LANES), F32), sin], axis=0))


def _pick_tile(pref, *lengths):
    t = pref
    while any(l % t for l in lengths):
        t //= 2
    return t


def kernel(x_prompt, x_sample, cache_nat_k, cache_nat_v, cache_mla_ckv, cache_mla_krope, c, c_ctx,
           norm1_g, norm2_g, w_ada, b_ada, w_in_ab, g_qn_a, g_kn_a, rpb_a, conv_b_w, w_out_ab,
           w_down_c, g_cq, g_ckv, w_uq_c, w_ukv_c, g_qn_c, g_kn_c, w_o_c, w_ffn_in, w_ffn_out):
    batch, seq, d = x_prompt.shape
    dbatch, dseq, _ = x_sample.shape
    depth = w_ada.shape[0]
    n_even = w_in_ab.shape[0]
    past = cache_nat_k.shape[3]
    nc, ns = batch * seq, dbatch * dseq
    n = nc + ns
    dff = w_ffn_out.shape[1]
    assert nc % dseq == 0 and dseq % seq == 0 and dseq % GRID_W == 0

    tm = _pick_tile(512, nc, dseq)
    tm_ab = _pick_tile(256, nc, dseq)
    tm_c = _pick_tile(512, nc, dseq)
    tm_f = _pick_tile(1024, nc, dseq)
    tf = _pick_tile(512, dff)
    assert tm % seq == 0

    x, x2, second = x_prompt.reshape(nc, d), x_sample.reshape(ns, d), "select"

    groups = 1 + dbatch
    gp = -(-groups // SUBLANES) * SUBLANES
    cond = jnp.concatenate([c_ctx[None, :], c, jnp.zeros((gp - groups, d), F32)], axis=0)
    mods_all = _adaln(cond, w_ada, b_ada).reshape(depth, gp, 6, d)

    rope_c, rope_s = _rope_tables(dseq, tm_c)
    ident_c = jnp.concatenate([jnp.ones((past, ROPE), F32), jnp.zeros((past, LANES - ROPE), F32)], axis=1)
    shape_kw = dict(nc=nc, dseq=dseq)

    w_in_ab_b = w_in_ab.astype(BF16)
    w_out_ab_b = w_out_ab.astype(BF16)
    w_o_c_b = w_o_c.astype(BF16)
    w_ffn_b = (w_ffn_in[0].astype(BF16), w_ffn_out[0].astype(BF16))

    nat = None
    mla_ckv, mla_kr = [], []
    for l in range(depth):
        i = l // 2
        mods = mods_all[l]
        if l % 2 == 0:
            gq = (g_qn_a[i] * (DH_A ** -0.5 * LOG2E))[None, :]
            qkv, gb, z, nk, nv = _inproj_ab(
                x, x2, second, mods, norm1_g[l][None, :], w_in_ab_b, gq, g_kn_a[i][None, :], nat, i,
                seq=seq, batch=batch, n_even=n_even, tm=tm_ab, **shape_kw)
            nat = (nk, nv)
            ya_c = _ctx_attn(qkv, qkv, qkv, (0, 1, 2), nc=nc, batch=batch, seq=seq,
                             heads=H_A, dq=DH_A, dv=DH_A)
            ya_s = _nat_attn(rpb_a[i], qkv, cache_nat_k, cache_nat_v, i, dbatch=dbatch, **shape_kw)
            x, h2 = _outproj(x, x2, second, mods, norm2_g[l][None, :], ya_c, ya_s, w_out_ab_b, i,
                             (gb, z, conv_b_w[i]), seq=seq, tm=tm, **shape_kw)
        else:
            wd = jnp.pad(w_down_c[i], ((0, 0), (0, LANES - ROPE))).astype(BF16)
            wq = jnp.pad(w_uq_c[i].reshape(Q_LORA, H_C, QK_DIM),
                         ((0, 0), (0, 0), (0, QK_PAD - QK_DIM))).reshape(Q_LORA, H_C * QK_PAD).astype(BF16)
            wkv = w_ukv_c[i].astype(BF16)
            gq = jnp.pad(g_qn_c[i] * (QK_DIM ** -0.5 * LOG2E), (0, QK_PAD - QK_DIM))[None, :]
            gkn = g_kn_c[i][None, :NOPE]
            gkr = jnp.pad(g_kn_c[i][NOPE:], (0, LANES - ROPE))[None, :]
            q, k, v, ckv_n, kr = _inproj_c(
                x, x2, second, mods, norm1_g[l][None, :], wd, g_cq[i][None, :], g_ckv[i][None, :],
                wq, gq, wkv, gkn, gkr, rope_c, rope_s, tm=tm_c, **shape_kw)
            mla_ckv.append(ckv_n.reshape(batch, seq, KV_LORA))
            mla_kr.append(kr.reshape(batch, seq, ROPE))
            kx, vx = _cache_kv(
                cache_mla_ckv[:, i].reshape(dbatch * past, KV_LORA),
                jnp.pad(cache_mla_krope[:, i].reshape(dbatch * past, ROPE), ((0, 0), (0, LANES - ROPE))),
                wkv, gkn, gkr, ident_c, jnp.zeros((past, LANES), F32), tm=past)
            ya_c = _ctx_attn(q, k, v, (0, 0, 0), nc=nc, batch=batch, seq=seq,
                             heads=H_C, dq=QK_PAD, dv=V_DIM)
            ya_s = _mla_attn(q, k, v, kx, vx, dbatch=dbatch, past=past,
                             tq=_pick_tile(512, dseq), **shape_kw)
            x, h2 = _outproj(x, x2, second, mods, norm2_g[l][None, :], ya_c, ya_s, w_o_c_b, i,
                             seq=seq, tm=tm, **shape_kw)
        last = l == depth - 1
        res = _ffn(x, h2, mods, w_ffn_b[0], w_ffn_b[1], None if last else (w_ffn_in, w_ffn_out), l,
                   last=last, tm=tm if last else tm_f, tf=tf, **shape_kw)
        if last:
            x2 = res
        else:
            x2, second, w_ffn_b = res[0], "add", res[1:]

    return (x2[0].reshape(batch, seq, d), x2[1].reshape(dbatch, dseq, d), nat[0], nat[1],
            jnp.stack(mla_ckv, axis=1), jnp.stack(mla_kr, axis=1))
```
